```python
import math
import jax, jax.numpy as jnp
from jax import lax
import numpy as np

D_MODEL = 1024
BATCH = 8
SEQ = 2048
DEPTH = 2
DEC_BATCH = 128
DEC_SEQ = 4
PAST_LEN = 2048
PAGE_SIZE = 128

N_A = max(1, DEPTH // 2)
N_B = DEPTH - N_A
A_KDIM = 128
A_HEADS = D_MODEL // A_KDIM
A_VDIM = D_MODEL // A_HEADS
A_HK = A_HEADS * A_KDIM
A_HV = A_HEADS * A_VDIM
HGRN_CHUNK = 64
SB_HEAD_DIM = 64
SB_HEADS = D_MODEL // SB_HEAD_DIM
SB_HD = SB_HEADS * SB_HEAD_DIM
SB_BLOCK = 128
SB_BIAS_INIT = -6.0
D_FF = ((8 * D_MODEL // 3 + 255) // 256) * 256
PLE_DIM = 256
EPS = 1e-6

kernel_name = "yoco_hgrn2_stickbreaking_step"


def rmsnorm(x, g):
    xf = x.astype(jnp.float32)
    y = xf * lax.rsqrt(jnp.mean(xf * xf, axis=-1, keepdims=True) + EPS)
    return y.astype(x.dtype) * g


def swiglu(xn, w_in, w_out):
    gu = xn @ w_in
    gate, up = gu[..., :D_FF], gu[..., D_FF:]
    return (jax.nn.silu(gate) * up) @ w_out


def ple_add(h, p, g, w_pe, w_pg):
    gate = jax.nn.sigmoid(rmsnorm(h, g) @ w_pg)
    return (p @ w_pe) * gate


def hgrn2_chunk_step(S, xs):
    q, k, lf, v = xs
    C = q.shape[1]
    G = jnp.cumsum(lf, axis=1)
    causal = jnp.tril(jnp.ones((C, C), dtype=bool))[None, :, :, None, None]
    diff = G[:, :, None] - G[:, None, :]
    dec = jnp.exp(jnp.where(causal, diff, -jnp.inf))
    att = jnp.einsum('bthk,bshk,btshk->bhts', q, k, dec)
    intra = jnp.einsum('bhts,bshv->bthv', att, v)
    inter = jnp.einsum('bthk,bhkv->bthv', q * jnp.exp(G), S)
    g_last = G[:, -1]
    k_dec = k * jnp.exp(g_last[:, None] - G)
    S_new = jnp.exp(g_last)[..., None] * S + jnp.einsum('bshk,bshv->bhkv', k_dec, v)
    return S_new, intra + inter


def hgrn2_mix(xn, S0, w_in, lb, g_onorm, w_o):
    B, L, _ = xn.shape
    proj = xn @ w_in
    q = proj[..., :A_HK]
    f = proj[..., A_HK:2 * A_HK]
    i = proj[..., 2 * A_HK:2 * A_HK + A_HV]
    og = proj[..., 2 * A_HK + A_HV:]
    q = jax.nn.silu(q.astype(jnp.float32)).reshape(B, L, A_HEADS, A_KDIM)
    fg = lb + (1.0 - lb) * jax.nn.sigmoid(f.astype(jnp.float32))
    lf = jnp.log(fg).reshape(B, L, A_HEADS, A_KDIM)
    kk = (1.0 - fg).reshape(B, L, A_HEADS, A_KDIM)
    v = i.astype(jnp.float32).reshape(B, L, A_HEADS, A_VDIM)
    C = HGRN_CHUNK if L % HGRN_CHUNK == 0 else L
    n = L // C

    def to_chunks(a):
        return a.reshape(B, n, C, A_HEADS, a.shape[-1]).transpose(1, 0, 2, 3, 4)

    S_fin, o = lax.scan(hgrn2_chunk_step, S0.astype(jnp.float32),
                        (to_chunks(q), to_chunks(kk), to_chunks(lf), to_chunks(v)))
    o = o.transpose(1, 0, 2, 3, 4).reshape(B, L, A_HEADS, A_VDIM)
    o = o * lax.rsqrt(jnp.mean(o * o, axis=-1, keepdims=True) + EPS)
    o = o * g_onorm.reshape(A_HEADS, A_VDIM).astype(jnp.float32)
    o = o.reshape(B, L, A_HV) * jax.nn.silu(og.astype(jnp.float32))
    return o.astype(xn.dtype) @ w_o, S_fin


def sb_block(q, k, v, bias, q_pos, k_pos):
    z = jnp.einsum('bqhd,bkhd->bhqk', q.astype(jnp.float32), k.astype(jnp.float32)) / math.sqrt(SB_HEAD_DIM)
    z = z + bias.astype(jnp.float32)[None, :, None, None]
    mask = (k_pos[None, :] < q_pos[:, None])[None, None]
    log_1mb = jnp.where(mask, jax.nn.log_sigmoid(-z), 0.0)
    tail = lax.cumsum(log_1mb, axis=3, reverse=True) - log_1mb
    log_a = jnp.where(mask, jax.nn.log_sigmoid(z) + tail, -jnp.inf)
    a = jnp.exp(log_a)
    o = jnp.einsum('bhqk,bkhd->bqhd', a, v.astype(jnp.float32))
    return o.astype(q.dtype)


def sb_attention(q, k, v, bias, q_pos, k_pos):
    B, Tq = q.shape[0], q.shape[1]
    if Tq % SB_BLOCK == 0 and Tq > SB_BLOCK:
        nb = Tq // SB_BLOCK
        qb = q.reshape(B, nb, SB_BLOCK, SB_HEADS, SB_HEAD_DIM).transpose(1, 0, 2, 3, 4)
        pb = q_pos.reshape(nb, SB_BLOCK)
        ob = lax.map(lambda a: sb_block(a[0], k, v, bias, a[1], k_pos), (qb, pb))
        return ob.transpose(1, 0, 2, 3, 4).reshape(B, Tq, SB_HEADS, SB_HEAD_DIM)
    return sb_block(q, k, v, bias, q_pos, k_pos)


def trunk(x, p, state_a, past_k, past_v, lbs, w_a_in, g_a_onorm, w_a_o, g_kv, w_kv, w_b_q, w_b_o, sb_bias,
          g_mix, g_ffn, w_ffn_in, w_ffn_out, g_ple, w_ple_in, w_ple_gate, g_final):
    B, L, _ = x.shape
    past_len = 0 if past_k is None else past_k.shape[1]
    q_pos = past_len + jnp.arange(L, dtype=jnp.int32)
    k_pos = jnp.arange(past_len + L, dtype=jnp.int32)
    h = x
    new_states = []
    k_new = v_new = k_all = v_all = None
    for layer in range(DEPTH):
        hn = rmsnorm(h, g_mix[layer])
        if layer < N_A:
            mix, s = hgrn2_mix(hn, state_a[layer], w_a_in[layer], lbs[layer], g_a_onorm[layer], w_a_o[layer])
            new_states.append(s)
        else:
            j = layer - N_A
            q = (hn @ w_b_q[j]).reshape(B, L, SB_HEADS, SB_HEAD_DIM)
            mix = sb_attention(q, k_all, v_all, sb_bias[j], q_pos, k_pos).reshape(B, L, SB_HD) @ w_b_o[j]
        h = h + mix
        h = h + swiglu(rmsnorm(h, g_ffn[layer]), w_ffn_in[layer], w_ffn_out[layer])
        h = h + ple_add(h, p[layer], g_ple[layer], w_ple_in[layer], w_ple_gate[layer])
        if layer == N_A - 1:
            kv = rmsnorm(h, g_kv) @ w_kv
            k_new = kv[..., :SB_HD].reshape(B, L, SB_HEADS, SB_HEAD_DIM)
            v_new = kv[..., SB_HD:].reshape(B, L, SB_HEADS, SB_HEAD_DIM)
            if past_k is None:
                k_all, v_all = k_new, v_new
            else:
                k_all = jnp.concatenate([past_k.astype(k_new.dtype), k_new], axis=1)
                v_all = jnp.concatenate([past_v.astype(v_new.dtype), v_new], axis=1)
    y = rmsnorm(h, g_final)
    return y, jnp.stack(new_states, axis=0), k_new, v_new


def setup_inputs(seed: int = 0) -> dict:
    key = jax.random.key(seed)
    ks = jax.random.split(key, 32)
    n_pages = PAST_LEN // PAGE_SIZE
    n_used = DEC_BATCH * n_pages
    n_pool = n_used + n_used // 4
    f32 = jnp.float32

    def nrm(k, shape, scale=1.0):
        return jax.random.normal(k, shape, f32) * scale

    def gain(k, shape):
        return 1.0 + 0.02 * jax.random.normal(k, shape, f32)

    perm = jax.random.permutation(ks[0], n_pool).astype(jnp.int32)
    page_table = perm[:n_used].reshape(DEC_BATCH, n_pages)
    return {
        "x_prompt": nrm(ks[1], (BATCH, SEQ, D_MODEL)),
        "x_sample": nrm(ks[2], (DEC_BATCH, DEC_SEQ, D_MODEL)),
        "p_prompt": nrm(ks[3], (DEPTH, BATCH, SEQ, PLE_DIM)),
        "p_sample": nrm(ks[4], (DEPTH, DEC_BATCH, DEC_SEQ, PLE_DIM)),
        "state_hgrn": nrm(ks[5], (N_A, DEC_BATCH, A_HEADS, A_KDIM, A_VDIM), 0.3),
        "cache_k": nrm(ks[6], (n_pool, PAGE_SIZE, SB_HEADS, SB_HEAD_DIM)),
        "cache_v": nrm(ks[7], (n_pool, PAGE_SIZE, SB_HEADS, SB_HEAD_DIM)),
        "page_table": page_table,
        "a_lb": nrm(ks[8], (N_A + 1, A_HK), 0.1),
        "w_a_in": nrm(ks[9], (N_A, D_MODEL, 2 * A_HK + 2 * A_HV), D_MODEL ** -0.5),
        "g_a_onorm": gain(ks[10], (N_A, A_HV)),
        "w_a_o": nrm(ks[11], (N_A, A_HV, D_MODEL), A_HV ** -0.5),
        "g_kv": gain(ks[12], (D_MODEL,)),
        "w_kv": nrm(ks[13], (D_MODEL, 2 * SB_HD), D_MODEL ** -0.5),
        "w_b_q": nrm(ks[14], (N_B, D_MODEL, SB_HD), D_MODEL ** -0.5),
        "w_b_o": nrm(ks[15], (N_B, SB_HD, D_MODEL), SB_HD ** -0.5),
        "sb_bias": SB_BIAS_INIT + nrm(ks[24], (N_B, SB_HEADS), 0.1),
        "g_mix": gain(ks[16], (DEPTH, D_MODEL)),
        "g_ffn": gain(ks[17], (DEPTH, D_MODEL)),
        "w_ffn_in": nrm(ks[18], (DEPTH, D_MODEL, 2 * D_FF), D_MODEL ** -0.5),
        "w_ffn_out": nrm(ks[19], (DEPTH, D_FF, D_MODEL), D_FF ** -0.5),
        "g_ple": gain(ks[20], (DEPTH, D_MODEL)),
        "w_ple_in": nrm(ks[21], (DEPTH, PLE_DIM, D_MODEL), PLE_DIM ** -0.5),
        "w_ple_gate": nrm(ks[22], (DEPTH, D_MODEL, D_MODEL), D_MODEL ** -0.5),
        "g_final": gain(ks[23], (D_MODEL,)),
    }


def reference(x_prompt, x_sample, p_prompt, p_sample, state_hgrn, cache_k, cache_v, page_table,
              a_lb, w_a_in, g_a_onorm, w_a_o, g_kv, w_kv, w_b_q, w_b_o, sb_bias, g_mix, g_ffn,
              w_ffn_in, w_ffn_out, g_ple, w_ple_in, w_ple_gate, g_final):
    lbs = jnp.cumsum(jax.nn.softmax(a_lb.astype(jnp.float32), axis=0), axis=0)[:N_A]

    s0_prompt = jnp.zeros((N_A, x_prompt.shape[0], A_HEADS, A_KDIM, A_VDIM), jnp.float32)
    y_prompt, st_prompt, k_prompt, v_prompt = trunk(
        x_prompt, p_prompt, s0_prompt, None, None, lbs, w_a_in, g_a_onorm, w_a_o, g_kv, w_kv,
        w_b_q, w_b_o, sb_bias, g_mix, g_ffn, w_ffn_in, w_ffn_out, g_ple, w_ple_in, w_ple_gate, g_final)

    db, n_pages = page_table.shape
    past_k = cache_k[page_table].reshape(db, n_pages * PAGE_SIZE, SB_HEADS, SB_HEAD_DIM)
    past_v = cache_v[page_table].reshape(db, n_pages * PAGE_SIZE, SB_HEADS, SB_HEAD_DIM)
    y_sample, st_sample, k_sample, v_sample = trunk(
        x_sample, p_sample, state_hgrn, past_k, past_v, lbs, w_a_in, g_a_onorm, w_a_o, g_kv, w_kv,
        w_b_q, w_b_o, sb_bias, g_mix, g_ffn, w_ffn_in, w_ffn_out, g_ple, w_ple_in, w_ple_gate, g_final)

    return (y_prompt, y_sample, st_prompt, st_sample, k_prompt, v_prompt, k_sample, v_sample)
```

```python
import functools

import jax
import jax.numpy as jnp
from jax import lax
from jax.experimental import pallas as pl
from jax.experimental.pallas import tpu as pltpu

F32 = jnp.float32
BF16 = jnp.bfloat16
EPS = 1e-6
HGRN_HEAD_DIM = 128
SB_HEAD_DIM = 64
LANES = 128
VMEM_LIMIT_BYTES = 56 * 1024 * 1024


def _sigmoid(x):
    return 1.0 / (1.0 + jnp.exp(-x))


def _rms(x, g):
    return x * lax.rsqrt(jnp.mean(x * x, axis=-1, keepdims=True) + EPS) * g


def _dot(a, b):
    return jnp.dot(a, b, preferred_element_type=F32)


def _dot_nt(a, b):
    return lax.dot_general(a, b, (((1,), (1,)), ((), ())), preferred_element_type=F32)


def _dot_tn(a, b):
    return lax.dot_general(a, b, (((0,), (0,)), ((), ())), preferred_element_type=F32)


def _split3(x):
    hi = x.astype(BF16)
    r = x - hi.astype(F32)
    mid = r.astype(BF16)
    lo = (r - mid.astype(F32)).astype(BF16)
    return hi, mid, lo


def _params(*sem):
    return pltpu.CompilerParams(dimension_semantics=sem, vmem_limit_bytes=VMEM_LIMIT_BYTES)


def _resident(shape, index_map):
    return pl.BlockSpec(shape, index_map, pipeline_mode=pl.Buffered(1))


def _hgrn_in_kernel(x_ref, g_ref, alb_ref, w_ref, q_ref, fg_ref, v_ref, og_ref, *, layer):
    d = x_ref.shape[1]
    hn = _rms(x_ref[...], g_ref[...]).astype(BF16)
    a = alb_ref[...]
    e = jnp.exp(a - jnp.max(a, axis=0, keepdims=True))
    lb = jnp.sum(e[0:layer + 1], axis=0, keepdims=True) / jnp.sum(e, axis=0, keepdims=True)
    pq = _dot(hn, w_ref[:, 0:d])
    q_ref[...] = pq * _sigmoid(pq)
    pf = _dot(hn, w_ref[:, d:2 * d])
    fg_ref[...] = lb + (1.0 - lb) * _sigmoid(pf)
    v_ref[...] = _dot(hn, w_ref[:, 2 * d:3 * d])
    po = _dot(hn, w_ref[:, 3 * d:4 * d])
    og_ref[...] = po * _sigmoid(po)


def _hgrn_in(x, g, a_lb, w_in, layer, tm):
    t, d = x.shape
    n_slots = a_lb.shape[0]
    row = pl.BlockSpec((tm, d), lambda i: (i, 0))
    out = jax.ShapeDtypeStruct((t, d), F32)
    return pl.pallas_call(
        functools.partial(_hgrn_in_kernel, layer=layer),
        grid=(t // tm,),
        in_specs=[row,
                  _resident((1, d), lambda i: (0, 0)),
                  _resident((n_slots, d), lambda i: (0, 0)),
                  _resident((None, d, 4 * d), lambda i: (layer, 0, 0))],
        out_specs=[row, row, row, row],
        out_shape=[out, out, out, out],
        compiler_params=_params("parallel"),
        name="hgrn_in",
    )(x, g, a_lb, w_in)


def _hgrn_scan_kernel(*refs, chunk, n_sub, has_init):
    if has_init:
        q_ref, fg_ref, v_ref, og_ref, gon_ref, s0_ref, o_ref, sout_ref, st_ref = refs
    else:
        q_ref, fg_ref, v_ref, og_ref, gon_ref, o_ref, sout_ref, st_ref = refs
        s0_ref = None
    n_heads = st_ref.shape[0]
    hd = HGRN_HEAD_DIM
    c = pl.program_id(1)

    @pl.when(c == 0)
    def _():
        if has_init:
            for h in range(n_heads):
                st_ref[h] = s0_ref[0, h].T
        else:
            st_ref[...] = jnp.zeros(st_ref.shape, F32)

    ri = lax.broadcasted_iota(jnp.int32, (chunk, chunk), 0)
    ci = lax.broadcasted_iota(jnp.int32, (chunk, chunk), 1)
    causal = ri >= ci
    tri = jnp.where(causal, 1.0, 0.0).astype(BF16)
    gon = gon_ref[...]

    for s in range(n_sub):
        rows = slice(s * chunk, (s + 1) * chunk)
        fg = fg_ref[0, rows, :]
        lf = jnp.log(fg)
        kk = 1.0 - fg
        hi, mid, lo = _split3(lf)
        gcum = _dot(tri, hi) + _dot(tri, mid) + _dot(tri, lo)
        g_last = gcum[chunk - 1:chunk, :]
        q_dec = (q_ref[0, rows, :] * jnp.exp(gcum)).astype(BF16)
        k_inv = (kk * jnp.exp(-gcum)).astype(BF16)
        k_dec = (kk * jnp.exp(g_last - gcum)).astype(BF16)
        s_dec = jnp.exp(g_last)
        v = v_ref[0, rows, :].astype(BF16)
        og = og_ref[0, rows, :]
        for h in range(n_heads):
            hs = slice(h * hd, (h + 1) * hd)
            st = st_ref[h]
            att = jnp.where(causal, _dot_nt(q_dec[:, hs], k_inv[:, hs]), 0.0)
            o = _dot(att.astype(BF16), v[:, hs]) + _dot_nt(q_dec[:, hs], st.astype(BF16))
            st_ref[h] = st * s_dec[:, hs] + _dot_tn(v[:, hs], k_dec[:, hs])
            on = o * lax.rsqrt(jnp.mean(o * o, axis=-1, keepdims=True) + EPS)
            o_ref[0, rows, hs] = (on * gon[:, hs] * og[:, hs]).astype(o_ref.dtype)

    @pl.when(c == pl.num_programs(1) - 1)
    def _():
        for h in range(n_heads):
            sout_ref[0, h] = st_ref[h].T


def _hgrn_scan(q, fg, v, og, g_onorm, s0, chunk, n_sub, out_dtype):
    b, l, d = q.shape
    n_heads = d // HGRN_HEAD_DIM
    rows = chunk * n_sub
    blk = pl.BlockSpec((1, rows, d), lambda i, c: (i, c, 0))
    st_blk = pl.BlockSpec((1, n_heads, HGRN_HEAD_DIM, HGRN_HEAD_DIM), lambda i, c: (i, 0, 0, 0))
    in_specs = [blk, blk, blk, blk, _resident((1, d), lambda i, c: (0, 0))]
    args = [q, fg, v, og, g_onorm]
    if s0 is not None:
        in_specs.append(st_blk)
        args.append(s0)
    return pl.pallas_call(
        functools.partial(_hgrn_scan_kernel, chunk=chunk, n_sub=n_sub, has_init=s0 is not None),
        grid=(b, l // rows),
        in_specs=in_specs,
        out_specs=[blk, st_blk],
        out_shape=[jax.ShapeDtypeStruct((b, l, d), out_dtype),
                   jax.ShapeDtypeStruct((b, n_heads, HGRN_HEAD_DIM, HGRN_HEAD_DIM), F32)],
        scratch_shapes=[pltpu.VMEM((n_heads, HGRN_HEAD_DIM, HGRN_HEAD_DIM), F32)],
        compiler_params=_params("parallel", "arbitrary"),
        name="hgrn_scan",
    )(*args)


def _layer_tail_kernel(*refs, d_ff, ff_chunk, final):
    (h_ref, a_ref, p_ref, wo_ref, gffn_ref, win_ref, wout_ref, gple_ref, wpe_ref, wpg_ref) = refs[:10]
    h1 = h_ref[...] + _dot(a_ref[...].astype(BF16), wo_ref[...])
    hn = _rms(h1, gffn_ref[...]).astype(BF16)
    h2 = h1
    for c0 in range(0, d_ff, ff_chunk):
        gate = _dot(hn, win_ref[:, c0:c0 + ff_chunk])
        up = _dot(hn, win_ref[:, d_ff + c0:d_ff + c0 + ff_chunk])
        act = (gate * _sigmoid(gate) * up).astype(BF16)
        h2 = h2 + _dot(act, wout_ref[c0:c0 + ff_chunk, :])
    pgate = _sigmoid(_dot(_rms(h2, gple_ref[...]).astype(BF16), wpg_ref[...]))
    h3 = h2 + _dot(p_ref[...].astype(BF16), wpe_ref[...]) * pgate
    if final:
        gfin_ref, y_ref = refs[10:]
        y_ref[...] = _rms(h3, gfin_ref[...])
    else:
        gkv_ref, wkv_ref, gq_ref, wq_ref, h_out_ref, k_ref, v_ref, q_ref = refs[10:]
        d = h3.shape[1]
        h_out_ref[...] = h3
        hkv = _rms(h3, gkv_ref[...]).astype(BF16)
        k_ref[...] = _dot(hkv, wkv_ref[:, 0:d])
        v_ref[...] = _dot(hkv, wkv_ref[:, d:2 * d])
        hq = _rms(h3, gq_ref[...]).astype(BF16)
        q_ref[...] = (_dot(hq, wq_ref[...]) * (SB_HEAD_DIM ** -0.5)).astype(q_ref.dtype)


def _layer_tail(h, a, p, layer, w, tm, final, q_dtype=BF16):
    t, d = h.shape
    ple = p.shape[-1]
    d_ff = w["w_ffn_out"].shape[1]
    row = pl.BlockSpec((tm, d), lambda i: (i, 0))
    vec = _resident((1, d), lambda i: (0, 0))

    def stacked(arr, idx):
        return _resident((None,) + arr.shape[1:], lambda i: (idx,) + (0,) * (arr.ndim - 1))

    j = layer - (w["w_ffn_in"].shape[0] - w["w_b_o"].shape[0])
    w_o = w["w_b_o"] if final else w["w_a_o"]
    in_specs = [row, row, pl.BlockSpec((None, tm, ple), lambda i: (layer, i, 0)),
                stacked(w_o, j if final else layer), vec, stacked(w["w_ffn_in"], layer),
                stacked(w["w_ffn_out"], layer), vec, stacked(w["w_ple_in"], layer), stacked(w["w_ple_gate"], layer)]
    args = [h, a, p, w_o, w["g_ffn"][layer][None], w["w_ffn_in"], w["w_ffn_out"], w["g_ple"][layer][None],
            w["w_ple_in"], w["w_ple_gate"]]
    if final:
        in_specs += [vec]
        args += [w["g_final"][None]]
        out_specs = [row]
        out_shape = [jax.ShapeDtypeStruct((t, d), F32)]
    else:
        in_specs += [vec, _resident(w["w_kv"].shape, lambda i: (0, 0)), vec, stacked(w["w_b_q"], 0)]
        args += [w["g_kv"][None], w["w_kv"], w["g_mix"][layer + 1][None], w["w_b_q"]]
        out_specs = [row, row, row, row]
        out_shape = [jax.ShapeDtypeStruct((t, d), F32)] * 3 + [jax.ShapeDtypeStruct((t, d), q_dtype)]
    return pl.pallas_call(
        functools.partial(_layer_tail_kernel, d_ff=d_ff, ff_chunk=d_ff // 2, final=final),
        grid=(t // tm,),
        in_specs=in_specs,
        out_specs=out_specs,
        out_shape=out_shape,
        compiler_params=_params("parallel"),
        name="layer_tail_final" if final else "layer_tail_kvq",
    )(*args)


def _strict_upper(n):
    ji = lax.broadcasted_iota(jnp.int32, (n, n), 0)
    si = lax.broadcasted_iota(jnp.int32, (n, n), 1)
    return jnp.where(ji > si, 1.0, 0.0).astype(BF16)


def _sb_block(z, upper, carry, visible):
    l1p = jnp.log1p(jnp.exp(-jnp.abs(z)))
    log_1mb = -(jnp.maximum(z, 0.0) + l1p)
    if visible is not None:
        log_1mb = jnp.where(visible, log_1mb, 0.0)
    tail = _dot(log_1mb.astype(BF16), upper)
    carry_b = jnp.concatenate([carry] * (z.shape[1] // LANES), axis=1)
    a = jnp.exp((jnp.minimum(z, 0.0) - l1p) + tail + carry_b)
    if visible is not None:
        a = jnp.where(visible, a, 0.0)
    block_sum = tail[:, 0:1] + log_1mb[:, 0:1]
    return a, carry + block_sum


def _sb_prompt_kernel(bias_ref, q_ref, k_ref, v_ref, o_ref, kb_ref, vb_ref, acc_ref, car_ref, *, blk):
    hp = pl.program_id(1)
    qi = pl.program_id(2)

    @pl.when(qi == 0)
    def _():
        kb_ref[...] = k_ref[0].astype(BF16)
        vb_ref[...] = v_ref[0].astype(BF16)

    q = q_ref[0]
    lane = lax.broadcasted_iota(jnp.int32, q.shape, 1)
    first = lane < SB_HEAD_DIM
    q_heads = (jnp.where(first, q, jnp.zeros_like(q)), jnp.where(first, jnp.zeros_like(q), q))
    upper = _strict_upper(blk)
    ti = lax.broadcasted_iota(jnp.int32, (blk, blk), 0)
    si = lax.broadcasted_iota(jnp.int32, (blk, blk), 1)
    strictly_before = si < ti
    acc_ref[...] = jnp.zeros(acc_ref.shape, F32)
    car_ref[...] = jnp.zeros(car_ref.shape, F32)

    def process(kb, visible):
        start = pl.multiple_of(kb * blk, blk)
        ks = kb_ref[pl.ds(start, blk), :]
        vs = vb_ref[pl.ds(start, blk), :]
        for j in range(2):
            z = _dot_nt(q_heads[j], ks) + bias_ref[2 * hp + j]
            a, car = _sb_block(z, upper, car_ref[j], visible)
            car_ref[j] = car
            acc_ref[j] += _dot(a.astype(BF16), vs)

    process(qi, strictly_before)

    def body(i, carry):
        process(qi - 1 - i, None)
        return carry

    lax.fori_loop(0, qi, body, 0)
    o_ref[0] = jnp.where(first[:, 0:LANES], acc_ref[0], acc_ref[1]).astype(o_ref.dtype)


def _sb_prompt(q, k, v, bias, blk):
    b, l, d = q.shape
    n_pairs = d // LANES
    q_blk = pl.BlockSpec((1, blk, LANES), lambda i, hp, qi, bias: (i, qi, hp))
    kv_blk = pl.BlockSpec((1, l, LANES), lambda i, hp, qi, bias: (i, 0, hp))
    return pl.pallas_call(
        functools.partial(_sb_prompt_kernel, blk=blk),
        grid_spec=pltpu.PrefetchScalarGridSpec(
            num_scalar_prefetch=1,
            grid=(b, n_pairs, l // blk),
            in_specs=[q_blk, kv_blk, kv_blk],
            out_specs=q_blk,
            scratch_shapes=[pltpu.VMEM((l, LANES), BF16), pltpu.VMEM((l, LANES), BF16),
                            pltpu.VMEM((2, blk, LANES), F32), pltpu.VMEM((2, blk, LANES), F32)]),
        out_shape=jax.ShapeDtypeStruct((b, l, d), BF16),
        compiler_params=_params("parallel", "parallel", "arbitrary"),
        name="sb_prompt",
    )(bias, q, k, v)


def _sb_sample_kernel(pt_ref, q_ref, kn_ref, vn_ref, bias_ref, *refs, n_step_pages, page, n_heads):
    k_refs = refs[:n_step_pages]
    v_refs = refs[n_step_pages:2 * n_step_pages]
    o_ref, qbd_ref, knb_ref, vnb_ref, acc_ref, car_ref = refs[2 * n_step_pages:]
    del pt_ref
    step = pl.program_id(1)
    n_q, d = q_ref.shape[1], q_ref.shape[2]
    rows = n_q * n_heads
    upper = _strict_upper(page)
    bias = bias_ref[...]

    def process(kb, vb, visible):
        z = _dot_nt(qbd_ref[...], kb) + bias
        a, car = _sb_block(z, upper, car_ref[...], visible)
        car_ref[...] = car
        acc_ref[...] += _dot(a.astype(BF16), vb)

    @pl.when(step == 0)
    def _():
        hrow = lax.broadcasted_iota(jnp.int32, (n_heads, d), 0)
        hlane = lax.broadcasted_iota(jnp.int32, (n_heads, d), 1) // SB_HEAD_DIM
        q = q_ref[0]
        for qi in range(n_q):
            qrow = jnp.broadcast_to(q[qi:qi + 1, :], (n_heads, d))
            qbd_ref[qi * n_heads:(qi + 1) * n_heads, :] = jnp.where(hrow == hlane, qrow, 0.0).astype(BF16)
        acc_ref[...] = jnp.zeros(acc_ref.shape, F32)
        car_ref[...] = jnp.zeros(car_ref.shape, F32)
        knb_ref[...] = jnp.zeros(knb_ref.shape, F32)
        vnb_ref[...] = jnp.zeros(vnb_ref.shape, F32)
        knb_ref[0:n_q, :] = kn_ref[0]
        vnb_ref[0:n_q, :] = vn_ref[0]
        r_q = lax.broadcasted_iota(jnp.int32, (rows, page), 0) // n_heads
        s_k = lax.broadcasted_iota(jnp.int32, (rows, page), 1)
        process(knb_ref[...].astype(BF16), vnb_ref[...].astype(BF16), s_k < r_q)

    for i in range(n_step_pages):
        process(k_refs[i][0].astype(BF16), v_refs[i][0].astype(BF16), None)

    @pl.when(step == pl.num_programs(1) - 1)
    def _():
        hrow = lax.broadcasted_iota(jnp.int32, (n_heads, d), 0)
        hlane = lax.broadcasted_iota(jnp.int32, (n_heads, d), 1) // SB_HEAD_DIM
        out_rows = []
        for qi in range(n_q):
            blk = acc_ref[qi * n_heads:(qi + 1) * n_heads, :]
            out_rows.append(jnp.sum(jnp.where(hrow == hlane, blk, 0.0), axis=0, keepdims=True))
        o_ref[0] = jnp.concatenate(out_rows, axis=0)


def _sb_sample(q, k_new, v_new, bias_rows, cache_k, cache_v, page_table, n_step_pages):
    b, n_q, d = q.shape
    n_pages = page_table.shape[1]
    page = cache_k.shape[1]
    n_heads = d // SB_HEAD_DIM
    rows = n_q * n_heads
    n_steps = n_pages // n_step_pages
    tok = pl.BlockSpec((1, n_q, d), lambda i, s, pt: (i, 0, 0))

    def page_spec(j):
        return pl.BlockSpec((1, page, d),
                            lambda i, s, pt: (pt[i * n_pages + (n_pages - 1 - (s * n_step_pages + j))], 0, 0))

    pages = [page_spec(j) for j in range(n_step_pages)]
    return pl.pallas_call(
        functools.partial(_sb_sample_kernel, n_step_pages=n_step_pages, page=page, n_heads=n_heads),
        grid_spec=pltpu.PrefetchScalarGridSpec(
            num_scalar_prefetch=1,
            grid=(b, n_steps),
            in_specs=[tok, tok, tok, pl.BlockSpec((rows, LANES), lambda i, s, pt: (0, 0))] + pages + pages,
            out_specs=tok,
            scratch_shapes=[pltpu.VMEM((rows, d), BF16), pltpu.VMEM((page, d), F32), pltpu.VMEM((page, d), F32),
                            pltpu.VMEM((rows, d), F32), pltpu.VMEM((rows, LANES), F32)]),
        out_shape=jax.ShapeDtypeStruct((b, n_q, d), F32),
        compiler_params=_params("parallel", "arbitrary"),
        name="sb_sample",
    )(page_table.reshape(-1), q, k_new, v_new, bias_rows, *([cache_k] * n_step_pages), *([cache_v] * n_step_pages))


HGRN_CHUNK = 64
HGRN_SUBCHUNKS = 4
SAMPLE_PAD = 8
PROMPT_BLOCK = 256
STEP_PAGES = 8


def _row_tile(t, want):
    return want if t % want == 0 else t


def _trunk(x, p, state0, past, w):
    b, l, d = x.shape
    t = b * l
    x2 = x.reshape(t, d)
    p2 = p.reshape(p.shape[0], t, p.shape[-1])
    q, fg, v, og = _hgrn_in(x2, w["g_mix"][0][None], w["a_lb"], w["w_a_in"], 0, _row_tile(t, 512))
    q, fg, v, og = (a.reshape(b, l, d) for a in (q, fg, v, og))
    if l % (HGRN_CHUNK * HGRN_SUBCHUNKS) == 0:
        o, s_fin = _hgrn_scan(q, fg, v, og, w["g_a_onorm"][0][None], state0, HGRN_CHUNK, HGRN_SUBCHUNKS, BF16)
    else:
        pad = ((0, 0), (0, SAMPLE_PAD - l), (0, 0))
        o, s_fin = _hgrn_scan(jnp.pad(q, pad), jnp.pad(fg, pad, constant_values=1.0), jnp.pad(v, pad),
                              jnp.pad(og, pad), w["g_a_onorm"][0][None], state0, SAMPLE_PAD, 1, F32)
        o = o[:, :l]
    tm = _row_tile(t, 256)
    h, k, v_kv, q1 = _layer_tail(x2, o.reshape(t, d), p2, 0, w, tm, final=False,
                                 q_dtype=BF16 if past is None else F32)
    if past is None:
        attn = _sb_prompt(q1.reshape(b, l, d), k.reshape(b, l, d), v_kv.reshape(b, l, d),
                          w["sb_bias"][0], PROMPT_BLOCK)
    else:
        cache_k, cache_v, page_table = past
        n_heads = d // SB_HEAD_DIM
        bias_rows = jnp.broadcast_to(jnp.tile(w["sb_bias"][0], l)[:, None], (l * n_heads, LANES))
        attn = _sb_sample(q1.reshape(b, l, d), k.reshape(b, l, d), v_kv.reshape(b, l, d), bias_rows,
                          cache_k.reshape(cache_k.shape[0], cache_k.shape[1], d),
                          cache_v.reshape(cache_v.shape[0], cache_v.shape[1], d), page_table,
                          min(STEP_PAGES, page_table.shape[1]))
    (y,) = _layer_tail(h, attn.reshape(t, d), p2, 1, w, tm, final=True)
    n_heads_kv = d // SB_HEAD_DIM
    return (y.reshape(b, l, d), s_fin[None], k.reshape(b, l, n_heads_kv, SB_HEAD_DIM),
            v_kv.reshape(b, l, n_heads_kv, SB_HEAD_DIM))


def kernel(x_prompt, x_sample, p_prompt, p_sample, state_hgrn, cache_k, cache_v, page_table, a_lb, w_a_in,
           g_a_onorm, w_a_o, g_kv, w_kv, w_b_q, w_b_o, sb_bias, g_mix, g_ffn, w_ffn_in, w_ffn_out, g_ple,
           w_ple_in, w_ple_gate, g_final):
    w = dict(a_lb=a_lb, g_a_onorm=g_a_onorm, g_kv=g_kv, sb_bias=sb_bias, g_mix=g_mix, g_ffn=g_ffn, g_ple=g_ple,
             g_final=g_final,
             w_a_in=w_a_in.astype(BF16), w_a_o=w_a_o.astype(BF16), w_kv=w_kv.astype(BF16),
             w_b_q=w_b_q.astype(BF16), w_b_o=w_b_o.astype(BF16), w_ffn_in=w_ffn_in.astype(BF16),
             w_ffn_out=w_ffn_out.astype(BF16), w_ple_in=w_ple_in.astype(BF16),
             w_ple_gate=w_ple_gate.astype(BF16))
    y_p, st_p, k_p, v_p = _trunk(x_prompt, p_prompt, None, None, w)
    y_s, st_s, k_s, v_s = _trunk(x_sample, p_sample, state_hgrn[0], (cache_k, cache_v, page_table), w)
    return (y_p, y_s, st_p, st_s, k_p, v_p, k_s, v_s)
```

```python
import functools

import jax
import jax.numpy as jnp
from jax import lax
from jax.experimental import pallas as pl
from jax.experimental.pallas import tpu as pltpu

F32 = jnp.float32
BF16 = jnp.bfloat16
EPS = 1e-6
LOG2E = 1.4426950408889634
HGRN_HEAD_DIM = 128
SB_HEAD_DIM = 64
LANES = 128
VMEM_LIMIT_BYTES = 56 * 1024 * 1024


def _sigmoid(x):
    return 1.0 / (1.0 + jnp.exp(-x))


def _rms(x, g):
    return x * lax.rsqrt(jnp.mean(x * x, axis=-1, keepdims=True) + EPS) * g


def _dot(a, b):
    return jnp.dot(a, b, preferred_element_type=F32)


def _dot_nt(a, b):
    return lax.dot_general(a, b, (((1,), (1,)), ((), ())), preferred_element_type=F32)


def _dot_tn(a, b):
    return lax.dot_general(a, b, (((0,), (0,)), ((), ())), preferred_element_type=F32)


def _split3(x):
    hi = x.astype(BF16)
    r = x - hi.astype(F32)
    mid = r.astype(BF16)
    lo = (r - mid.astype(F32)).astype(BF16)
    return hi, mid, lo


def _params(*sem):
    return pltpu.CompilerParams(dimension_semantics=sem, vmem_limit_bytes=VMEM_LIMIT_BYTES)


def _resident(shape, index_map):
    return pl.BlockSpec(shape, index_map, pipeline_mode=pl.Buffered(1))


def _hgrn_in_kernel(x_ref, g_ref, alb_ref, w_ref, q_ref, fg_ref, v_ref, og_ref, *, layer):
    d = x_ref.shape[1]
    hn = _rms(x_ref[...], g_ref[...]).astype(BF16)
    a = alb_ref[...]
    e = jnp.exp(a - jnp.max(a, axis=0, keepdims=True))
    lb = jnp.sum(e[0:layer + 1], axis=0, keepdims=True) / jnp.sum(e, axis=0, keepdims=True)
    pq = _dot(hn, w_ref[:, 0:d])
    q_ref[...] = pq * _sigmoid(pq)
    pf = _dot(hn, w_ref[:, d:2 * d])
    fg_ref[...] = lb + (1.0 - lb) * _sigmoid(pf)
    v_ref[...] = _dot(hn, w_ref[:, 2 * d:3 * d])
    po = _dot(hn, w_ref[:, 3 * d:4 * d])
    og_ref[...] = po * _sigmoid(po)


def _hgrn_in(x, g, a_lb, w_in, layer, tm):
    t, d = x.shape
    n_slots = a_lb.shape[0]
    row = pl.BlockSpec((tm, d), lambda i: (i, 0))
    out = jax.ShapeDtypeStruct((t, d), F32)
    return pl.pallas_call(
        functools.partial(_hgrn_in_kernel, layer=layer),
        grid=(t // tm,),
        in_specs=[row,
                  _resident((1, d), lambda i: (0, 0)),
                  _resident((n_slots, d), lambda i: (0, 0)),
                  _resident((None, d, 4 * d), lambda i: (layer, 0, 0))],
        out_specs=[row, row, row, row],
        out_shape=[out, out, out, out],
        compiler_params=_params("parallel"),
        name="hgrn_in",
    )(x, g, a_lb, w_in)


def _hgrn_scan_kernel(*refs, chunk, n_sub, has_init):
    if has_init:
        q_ref, fg_ref, v_ref, og_ref, gon_ref, s0_ref, o_ref, sout_ref, st_ref = refs
    else:
        q_ref, fg_ref, v_ref, og_ref, gon_ref, o_ref, sout_ref, st_ref = refs
        s0_ref = None
    n_heads = st_ref.shape[0]
    hd = HGRN_HEAD_DIM
    c = pl.program_id(1)

    @pl.when(c == 0)
    def _():
        if has_init:
            for h in range(n_heads):
                st_ref[h] = s0_ref[0, h].T
        else:
            st_ref[...] = jnp.zeros(st_ref.shape, F32)

    ri = lax.broadcasted_iota(jnp.int32, (chunk, chunk), 0)
    ci = lax.broadcasted_iota(jnp.int32, (chunk, chunk), 1)
    causal = ri >= ci
    tri = jnp.where(causal, 1.0, 0.0).astype(BF16)
    gon = gon_ref[...]

    for s in range(n_sub):
        rows = slice(s * chunk, (s + 1) * chunk)
        fg = fg_ref[0, rows, :]
        lf = jnp.log(fg)
        kk = 1.0 - fg
        hi, mid, lo = _split3(lf)
        gcum = _dot(tri, hi) + _dot(tri, mid) + _dot(tri, lo)
        g_last = gcum[chunk - 1:chunk, :]
        q_dec = (q_ref[0, rows, :] * jnp.exp(gcum)).astype(BF16)
        k_inv = (kk * jnp.exp(-gcum)).astype(BF16)
        k_dec = (kk * jnp.exp(g_last - gcum)).astype(BF16)
        s_dec = jnp.exp(g_last)
        v = v_ref[0, rows, :].astype(BF16)
        og = og_ref[0, rows, :]

        def first_products(h, _, rows=rows, q_dec=q_dec, k_inv=k_inv, k_dec=k_dec, s_dec=s_dec, v=v):
            hs = slice(h * hd, (h + 1) * hd)
            st = st_ref[h]
            att = _dot_nt(q_dec[:, hs], k_inv[:, hs])
            inter = _dot_nt(q_dec[:, hs], st.astype(BF16))
            st_ref[h] = st * s_dec[:, hs] + _dot_tn(v[:, hs], k_dec[:, hs])
            return att, inter

        def outputs(h, prods, rows=rows, v=v, og=og):
            hs = slice(h * hd, (h + 1) * hd)
            att, inter = prods
            o = _dot(jnp.where(causal, att, 0.0).astype(BF16), v[:, hs]) + inter
            on = o * lax.rsqrt(jnp.mean(o * o, axis=-1, keepdims=True) + EPS)
            o_ref[0, rows, hs] = (on * gon[:, hs] * og[:, hs]).astype(o_ref.dtype)

        _emit_skewed(n_heads, [first_products, outputs])

    @pl.when(c == pl.num_programs(1) - 1)
    def _():
        for h in range(n_heads):
            sout_ref[0, h] = st_ref[h].T


def _hgrn_scan(q, fg, v, og, g_onorm, s0, chunk, n_sub, out_dtype):
    b, l, d = q.shape
    n_heads = d // HGRN_HEAD_DIM
    rows = chunk * n_sub
    blk = pl.BlockSpec((1, rows, d), lambda i, c: (i, c, 0))
    st_blk = pl.BlockSpec((1, n_heads, HGRN_HEAD_DIM, HGRN_HEAD_DIM), lambda i, c: (i, 0, 0, 0))
    in_specs = [blk, blk, blk, blk, _resident((1, d), lambda i, c: (0, 0))]
    args = [q, fg, v, og, g_onorm]
    if s0 is not None:
        in_specs.append(st_blk)
        args.append(s0)
    return pl.pallas_call(
        functools.partial(_hgrn_scan_kernel, chunk=chunk, n_sub=n_sub, has_init=s0 is not None),
        grid=(b, l // rows),
        in_specs=in_specs,
        out_specs=[blk, st_blk],
        out_shape=[jax.ShapeDtypeStruct((b, l, d), out_dtype),
                   jax.ShapeDtypeStruct((b, n_heads, HGRN_HEAD_DIM, HGRN_HEAD_DIM), F32)],
        scratch_shapes=[pltpu.VMEM((n_heads, HGRN_HEAD_DIM, HGRN_HEAD_DIM), F32)],
        compiler_params=_params("parallel", "arbitrary"),
        name="hgrn_scan",
    )(*args)


def _layer_tail_kernel(*refs, d_ff, ff_chunk, final):
    (h_ref, a_ref, p_ref, wo_ref, gffn_ref, win_ref, wout_ref, gple_ref, wpe_ref, wpg_ref) = refs[:10]
    h1 = h_ref[...] + _dot(a_ref[...].astype(BF16), wo_ref[...])
    hn = _rms(h1, gffn_ref[...]).astype(BF16)
    h2 = h1
    for c0 in range(0, d_ff, ff_chunk):
        gate = _dot(hn, win_ref[:, c0:c0 + ff_chunk])
        up = _dot(hn, win_ref[:, d_ff + c0:d_ff + c0 + ff_chunk])
        act = (gate * _sigmoid(gate) * up).astype(BF16)
        h2 = h2 + _dot(act, wout_ref[c0:c0 + ff_chunk, :])
    pgate = _sigmoid(_dot(_rms(h2, gple_ref[...]).astype(BF16), wpg_ref[...]))
    h3 = h2 + _dot(p_ref[...].astype(BF16), wpe_ref[...]) * pgate
    if final:
        gfin_ref, y_ref = refs[10:]
        y_ref[...] = _rms(h3, gfin_ref[...])
    else:
        gkv_ref, wkv_ref, gq_ref, wq_ref, h_out_ref, k_ref, v_ref, q_ref = refs[10:18]
        d = h3.shape[1]
        h_out_ref[...] = h3
        hkv = _rms(h3, gkv_ref[...]).astype(BF16)
        k = _dot(hkv, wkv_ref[:, 0:d])
        v = _dot(hkv, wkv_ref[:, d:2 * d])
        k_ref[...] = k
        v_ref[...] = v
        if len(refs) > 18:
            kb_ref, vb_ref = refs[18:]
            kb_ref[...] = k.astype(BF16)
            vb_ref[...] = v.astype(BF16)
        hq = _rms(h3, gq_ref[...]).astype(BF16)
        q_ref[...] = (_dot(hq, wq_ref[...]) * (LOG2E * SB_HEAD_DIM ** -0.5)).astype(q_ref.dtype)


def _layer_tail(h, a, p, layer, w, tm, final, q_dtype=BF16, kv_copies=False):
    t, d = h.shape
    ple = p.shape[-1]
    d_ff = w["w_ffn_out"].shape[1]
    row = pl.BlockSpec((tm, d), lambda i: (i, 0))
    vec = _resident((1, d), lambda i: (0, 0))

    def stacked(arr, idx):
        return _resident((None,) + arr.shape[1:], lambda i: (idx,) + (0,) * (arr.ndim - 1))

    j = layer - (w["w_ffn_in"].shape[0] - w["w_b_o"].shape[0])
    w_o = w["w_b_o"] if final else w["w_a_o"]
    in_specs = [row, row, pl.BlockSpec((None, tm, ple), lambda i: (layer, i, 0)),
                stacked(w_o, j if final else layer), vec, stacked(w["w_ffn_in"], layer),
                stacked(w["w_ffn_out"], layer), vec, stacked(w["w_ple_in"], layer), stacked(w["w_ple_gate"], layer)]
    args = [h, a, p, w_o, w["g_ffn"][layer][None], w["w_ffn_in"], w["w_ffn_out"], w["g_ple"][layer][None],
            w["w_ple_in"], w["w_ple_gate"]]
    if final:
        in_specs += [vec]
        args += [w["g_final"][None]]
        out_specs = [row]
        out_shape = [jax.ShapeDtypeStruct((t, d), F32)]
    else:
        in_specs += [vec, _resident(w["w_kv"].shape, lambda i: (0, 0)), vec, stacked(w["w_b_q"], 0)]
        args += [w["g_kv"][None], w["w_kv"], w["g_mix"][layer + 1][None], w["w_b_q"]]
        out_specs = [row, row, row, row] + [row, row] * kv_copies
        out_shape = ([jax.ShapeDtypeStruct((t, d), F32)] * 3 + [jax.ShapeDtypeStruct((t, d), q_dtype)]
                     + [jax.ShapeDtypeStruct((t, d), BF16)] * (2 * kv_copies))
    return pl.pallas_call(
        functools.partial(_layer_tail_kernel, d_ff=d_ff, ff_chunk=d_ff // 2, final=final),
        grid=(t // tm,),
        in_specs=in_specs,
        out_specs=out_specs,
        out_shape=out_shape,
        compiler_params=_params("parallel"),
        name="layer_tail_final" if final else "layer_tail_kvq",
    )(*args)


def _strict_upper(n):
    ji = lax.broadcasted_iota(jnp.int32, (n, n), 0)
    si = lax.broadcasted_iota(jnp.int32, (n, n), 1)
    return jnp.where(ji > si, 1.0, 0.0).astype(BF16)


def _sb_logs(z, visible):
    m = jnp.minimum(z, 0.0)
    d = m - z
    l1p = jnp.log2(1.0 + jnp.exp2(m + d))
    log_1mb = d - l1p
    if visible is not None:
        log_1mb = jnp.where(visible, log_1mb, 0.0)
    return m - l1p, log_1mb


def _sb_weights(log_b, log_1mb, tail, carry, visible):
    carry_b = jnp.concatenate([carry] * (log_b.shape[1] // LANES), axis=1)
    a = jnp.exp2(log_b + tail + carry_b)
    if visible is not None:
        a = jnp.where(visible, a, 0.0)
    return a, carry + (tail[:, 0:1] + log_1mb[:, 0:1])


def _sb_block(z, upper, carry, visible):
    log_b, log_1mb = _sb_logs(z, visible)
    tail = _dot(log_1mb.astype(BF16), upper)
    return _sb_weights(log_b, log_1mb, tail, carry, visible)


def _emit_skewed(n_items, stages):
    state = [None] * n_items
    for t in range(n_items + len(stages) - 1):
        for s in reversed(range(len(stages))):
            i = t - s
            if 0 <= i < n_items:
                state[i] = stages[s](i, state[i])


def _sb_prompt_kernel(bias_ref, q_ref, k_ref, v_ref, o_ref, qh_ref, acc_ref, car_ref, *, blk, n_heads):
    hg = pl.program_id(1)
    qi = pl.program_id(2)

    lane = lax.broadcasted_iota(jnp.int32, (blk, LANES), 1)
    first = lane < SB_HEAD_DIM
    for j in range(n_heads):
        qt = q_ref[0, :, (j // 2) * LANES:(j // 2 + 1) * LANES]
        qh_ref[j] = jnp.where(first if j % 2 == 0 else ~first, qt, jnp.zeros_like(qt))
    upper = _strict_upper(blk)
    ti = lax.broadcasted_iota(jnp.int32, (blk, blk), 0)
    si = lax.broadcasted_iota(jnp.int32, (blk, blk), 1)
    strictly_before = si < ti
    acc_ref[...] = jnp.zeros(acc_ref.shape, F32)
    car_ref[...] = jnp.zeros(car_ref.shape, F32)

    def process(kb, visible):
        start = pl.multiple_of(kb * blk, blk)

        def tile(j):
            return slice((j // 2) * LANES, (j // 2 + 1) * LANES)

        def logs(j, _):
            z = _dot_nt(qh_ref[j], k_ref[0, pl.ds(start, blk), tile(j)]) + bias_ref[n_heads * hg + j]
            return _sb_logs(z, visible)

        def tails(j, st):
            return st + (_dot(st[1].astype(BF16), upper),)

        def weights(j, st):
            a, car_ref[j] = _sb_weights(*st, car_ref[j], visible)
            acc_ref[j] += _dot(a.astype(BF16), v_ref[0, pl.ds(start, blk), tile(j)])

        _emit_skewed(n_heads, [logs, tails, weights])

    process(qi, strictly_before)

    def body(i, carry):
        process(qi - 1 - i, None)
        return carry

    lax.fori_loop(0, qi, body, 0)
    for t in range(n_heads // 2):
        o_ref[0, :, t * LANES:(t + 1) * LANES] = jnp.where(first, acc_ref[2 * t], acc_ref[2 * t + 1]).astype(o_ref.dtype)


def _sb_prompt(q, k, v, bias, blk, n_heads):
    b, l, d = q.shape
    width = n_heads * SB_HEAD_DIM
    q_blk = pl.BlockSpec((1, blk, width), lambda i, hg, qi, bias: (i, qi, hg))
    kv_blk = pl.BlockSpec((1, l, width), lambda i, hg, qi, bias: (i, 0, hg))
    return pl.pallas_call(
        functools.partial(_sb_prompt_kernel, blk=blk, n_heads=n_heads),
        grid_spec=pltpu.PrefetchScalarGridSpec(
            num_scalar_prefetch=1,
            grid=(b, d // width, l // blk),
            in_specs=[q_blk, kv_blk, kv_blk],
            out_specs=q_blk,
            scratch_shapes=[pltpu.VMEM((n_heads, blk, LANES), BF16),
                            pltpu.VMEM((n_heads, blk, LANES), F32), pltpu.VMEM((n_heads, blk, LANES), F32)]),
        out_shape=jax.ShapeDtypeStruct((b, l, d), BF16),
        compiler_params=_params("parallel", "parallel", "arbitrary"),
        name="sb_prompt",
    )(bias, q, k, v)


def _sb_sample_kernel(pt_ref, q_ref, kn_ref, vn_ref, bias_ref, fold_ref, spread_ref, *refs,
                      n_step_pages, page, n_heads):
    k_refs = refs[:n_step_pages]
    v_refs = refs[n_step_pages:2 * n_step_pages]
    o_ref, kn_buf, vn_buf, acc_ref, car_ref = refs[2 * n_step_pages:]
    del pt_ref
    step = pl.program_id(1)
    rows = q_ref.shape[1]
    cols = page * n_heads
    group = 2 * LANES
    n_slab = cols // group
    pos_per_slab = group // n_heads
    upper = _strict_upper(page)
    bias = bias_ref[...]
    fold = fold_ref[...]
    spread = spread_ref[...]
    q = q_ref[0].astype(BF16)
    r_head = lax.broadcasted_iota(jnp.int32, (rows, group), 0) % n_heads
    c_head = lax.broadcasted_iota(jnp.int32, (rows, group), 1) % n_heads
    own_head = r_head == c_head
    lane_slab = lax.broadcasted_iota(jnp.int32, (rows, LANES), 1) // pos_per_slab

    def run(keys, values, visible):
        flow = {"carry": car_ref[...], "acc": acc_ref[...]}

        def scores(i, _):
            s = _dot_nt(q, keys[i]())
            hi_parts, lo_parts = [], []
            for t in range(n_slab):
                part = jnp.where(own_head, s[:, t * group:(t + 1) * group], 0.0)
                hi = part.astype(BF16)
                hi_parts.append(hi)
                lo_parts.append((part - hi.astype(F32)).astype(BF16))
            return jnp.concatenate(hi_parts, axis=0), jnp.concatenate(lo_parts, axis=0)

        def logs(i, st):
            folded = _dot(st[0], fold) + _dot(st[1], fold)
            z = jnp.zeros((rows, LANES), F32)
            for t in range(n_slab):
                z = z + jnp.where(lane_slab == t, folded[t * rows:(t + 1) * rows, :], 0.0)
            return _sb_logs(z + bias, visible)

        def tails(i, st):
            return st + (_dot(st[1].astype(BF16), upper),)

        def weights(i, st):
            a, flow["carry"] = _sb_weights(*st, flow["carry"], visible)
            a = a.astype(BF16)
            a_slabs = jnp.concatenate([jnp.where(lane_slab == t, a, jnp.zeros_like(a)) for t in range(n_slab)],
                                      axis=0)
            return _dot(a_slabs, spread)

        def values_product(i, wide):
            a_cols = jnp.concatenate([jnp.where(own_head, wide[t * rows:(t + 1) * rows, :], 0.0).astype(BF16)
                                      for t in range(n_slab)], axis=1)
            flow["acc"] = flow["acc"] + _dot(a_cols, values[i]())

        _emit_skewed(len(keys), [scores, logs, tails, weights, values_product])
        car_ref[...] = flow["carry"]
        acc_ref[...] = flow["acc"]

    @pl.when(step == 0)
    def _():
        acc_ref[...] = jnp.zeros(acc_ref.shape, F32)
        car_ref[...] = jnp.zeros(car_ref.shape, F32)
        kn_buf[...] = jnp.zeros(kn_buf.shape, F32)
        vn_buf[...] = jnp.zeros(vn_buf.shape, F32)
        kn_buf[0:kn_ref.shape[1], :] = kn_ref[0]
        vn_buf[0:vn_ref.shape[1], :] = vn_ref[0]
        r_q = lax.broadcasted_iota(jnp.int32, (rows, page), 0) // n_heads
        s_k = lax.broadcasted_iota(jnp.int32, (rows, page), 1)
        run([lambda: kn_buf[...].astype(BF16)], [lambda: vn_buf[...].astype(BF16)], s_k < r_q)

    def page_operand(ref):
        return lambda: ref[0].reshape(cols, SB_HEAD_DIM).astype(BF16)

    run([page_operand(r) for r in k_refs], [page_operand(r) for r in v_refs], None)

    @pl.when(step == pl.num_programs(1) - 1)
    def _():
        o_ref[0] = acc_ref[...]


def _sb_sample(q, k_new, v_new, bias_rows, cache_k, cache_v, page_table, n_step_pages):
    b, rows, hd = q.shape
    n_pages = page_table.shape[1]
    page, n_heads = cache_k.shape[1], cache_k.shape[2]
    n_steps = n_pages // n_step_pages
    group = 2 * LANES
    c_pos = jnp.arange(group)[:, None] // n_heads
    j_pos = jnp.arange(LANES)[None, :] % (group // n_heads)
    fold = (c_pos == j_pos).astype(BF16)
    tok = pl.BlockSpec((1, rows, hd), lambda i, s, pt: (i, 0, 0))

    def const(arr):
        return _resident(arr.shape, lambda i, s, pt: (0,) * arr.ndim)

    def page_spec(j):
        return pl.BlockSpec((1, page, n_heads, hd),
                            lambda i, s, pt: (pt[i * n_pages + (n_pages - 1 - (s * n_step_pages + j))], 0, 0, 0))

    pages = [page_spec(j) for j in range(n_step_pages)]
    return pl.pallas_call(
        functools.partial(_sb_sample_kernel, n_step_pages=n_step_pages, page=page, n_heads=n_heads),
        grid_spec=pltpu.PrefetchScalarGridSpec(
            num_scalar_prefetch=1,
            grid=(b, n_steps),
            in_specs=[tok, tok, tok, const(bias_rows), const(fold), const(fold.T)] + pages + pages,
            out_specs=tok,
            scratch_shapes=[pltpu.VMEM((page * n_heads, hd), F32), pltpu.VMEM((page * n_heads, hd), F32),
                            pltpu.VMEM((rows, hd), F32), pltpu.VMEM((rows, LANES), F32)]),
        out_shape=jax.ShapeDtypeStruct((b, rows, hd), F32),
        compiler_params=_params("parallel", "arbitrary"),
        name="sb_sample",
    )(page_table.reshape(-1), q, k_new, v_new, bias_rows, fold, fold.T,
      *([cache_k] * n_step_pages), *([cache_v] * n_step_pages))


HGRN_CHUNK = 64
HGRN_SUBCHUNKS = 4
SAMPLE_PAD = 8
PROMPT_BLOCK = 256
PROMPT_HEADS = 16
STEP_PAGES = 8


def _row_tile(t, want):
    return want if t % want == 0 else t


def _trunk(x, p, state0, past, w):
    b, l, d = x.shape
    t = b * l
    x2 = x.reshape(t, d)
    p2 = p.reshape(p.shape[0], t, p.shape[-1])
    q, fg, v, og = _hgrn_in(x2, w["g_mix"][0][None], w["a_lb"], w["w_a_in"], 0, _row_tile(t, 512))
    q, fg, v, og = (a.reshape(b, l, d) for a in (q, fg, v, og))
    if l % (HGRN_CHUNK * HGRN_SUBCHUNKS) == 0:
        o, s_fin = _hgrn_scan(q, fg, v, og, w["g_a_onorm"][0][None], state0, HGRN_CHUNK, HGRN_SUBCHUNKS, BF16)
    else:
        pad = ((0, 0), (0, SAMPLE_PAD - l), (0, 0))
        o, s_fin = _hgrn_scan(jnp.pad(q, pad), jnp.pad(fg, pad, constant_values=1.0), jnp.pad(v, pad),
                              jnp.pad(og, pad), w["g_a_onorm"][0][None], state0, SAMPLE_PAD, 1, F32)
        o = o[:, :l]
    tm = _row_tile(t, 256)
    if past is None:
        h, k, v_kv, q1, kb, vb = _layer_tail(x2, o.reshape(t, d), p2, 0, w, tm, final=False, kv_copies=True)
        attn = _sb_prompt(q1.reshape(b, l, d), kb.reshape(b, l, d), vb.reshape(b, l, d),
                          w["sb_bias"][0], PROMPT_BLOCK, PROMPT_HEADS)
    else:
        h, k, v_kv, q1 = _layer_tail(x2, o.reshape(t, d), p2, 0, w, tm, final=False, q_dtype=F32)
        cache_k, cache_v, page_table = past
        n_heads = d // SB_HEAD_DIM
        bias_rows = jnp.broadcast_to(jnp.tile(w["sb_bias"][0], l)[:, None], (l * n_heads, LANES))
        heads_shape = (b, l * n_heads, SB_HEAD_DIM)
        attn = _sb_sample(q1.reshape(heads_shape), k.reshape(heads_shape), v_kv.reshape(heads_shape), bias_rows,
                          cache_k, cache_v, page_table, min(STEP_PAGES, page_table.shape[1]))
    (y,) = _layer_tail(h, attn.reshape(t, d), p2, 1, w, tm, final=True)
    n_heads_kv = d // SB_HEAD_DIM
    return (y.reshape(b, l, d), s_fin[None], k.reshape(b, l, n_heads_kv, SB_HEAD_DIM),
            v_kv.reshape(b, l, n_heads_kv, SB_HEAD_DIM))


def kernel(x_prompt, x_sample, p_prompt, p_sample, state_hgrn, cache_k, cache_v, page_table, a_lb, w_a_in,
           g_a_onorm, w_a_o, g_kv, w_kv, w_b_q, w_b_o, sb_bias, g_mix, g_ffn, w_ffn_in, w_ffn_out, g_ple,
           w_ple_in, w_ple_gate, g_final):
    w = dict(a_lb=a_lb, g_a_onorm=g_a_onorm, g_kv=g_kv, sb_bias=sb_bias * LOG2E, g_mix=g_mix, g_ffn=g_ffn, g_ple=g_ple,
             g_final=g_final,
             w_a_in=w_a_in.astype(BF16), w_a_o=w_a_o.astype(BF16), w_kv=w_kv.astype(BF16),
             w_b_q=w_b_q.astype(BF16), w_b_o=w_b_o.astype(BF16), w_ffn_in=w_ffn_in.astype(BF16),
             w_ffn_out=w_ffn_out.astype(BF16), w_ple_in=w_ple_in.astype(BF16),
             w_ple_gate=w_ple_gate.astype(BF16))
    y_p, st_p, k_p, v_p = _trunk(x_prompt, p_prompt, None, None, w)
    y_s, st_s, k_s, v_s = _trunk(x_sample, p_sample, state_hgrn[0], (cache_k, cache_v, page_table), w)
    return (y_p, y_s, st_p, st_s, k_p, v_p, k_s, v_s)
```

```python
import functools

import jax
import jax.numpy as jnp
from jax import lax
from jax.experimental import pallas as pl
from jax.experimental.pallas import tpu as pltpu

F32 = jnp.float32
BF16 = jnp.bfloat16
EPS = 1e-6
LOG2E = 1.4426950408889634
HGRN_HEAD_DIM = 128
SB_HEAD_DIM = 64
LANES = 128
VMEM_LIMIT_BYTES = 56 * 1024 * 1024


def _sigmoid(x):
    return 1.0 / (1.0 + jnp.exp(-x))


def _rms(x, g):
    return x * lax.rsqrt(jnp.mean(x * x, axis=-1, keepdims=True) + EPS) * g


def _dot(a, b):
    return jnp.dot(a, b, preferred_element_type=F32)


def _dot_nt(a, b):
    return lax.dot_general(a, b, (((1,), (1,)), ((), ())), preferred_element_type=F32)


def _dot_tn(a, b):
    return lax.dot_general(a, b, (((0,), (0,)), ((), ())), preferred_element_type=F32)


def _split3(x):
    hi = x.astype(BF16)
    r = x - hi.astype(F32)
    mid = r.astype(BF16)
    lo = (r - mid.astype(F32)).astype(BF16)
    return hi, mid, lo


def _params(*sem):
    return pltpu.CompilerParams(dimension_semantics=sem, vmem_limit_bytes=VMEM_LIMIT_BYTES)


def _resident(shape, index_map):
    return pl.BlockSpec(shape, index_map, pipeline_mode=pl.Buffered(1))


def _hgrn_in_kernel(x_ref, g_ref, alb_ref, w_ref, q_ref, fg_ref, v_ref, og_ref, *, layer):
    d = x_ref.shape[1]
    hn = _rms(x_ref[...], g_ref[...]).astype(BF16)
    a = alb_ref[...]
    e = jnp.exp(a - jnp.max(a, axis=0, keepdims=True))
    lb = jnp.sum(e[0:layer + 1], axis=0, keepdims=True) / jnp.sum(e, axis=0, keepdims=True)
    pq = _dot(hn, w_ref[:, 0:d])
    q_ref[...] = pq * _sigmoid(pq)
    pf = _dot(hn, w_ref[:, d:2 * d])
    fg_ref[...] = lb + (1.0 - lb) * _sigmoid(pf)
    v_ref[...] = _dot(hn, w_ref[:, 2 * d:3 * d])
    po = _dot(hn, w_ref[:, 3 * d:4 * d])
    og_ref[...] = po * _sigmoid(po)


def _hgrn_in(x, g, a_lb, w_in, layer, tm):
    t, d = x.shape
    n_slots = a_lb.shape[0]
    row = pl.BlockSpec((tm, d), lambda i: (i, 0))
    out = jax.ShapeDtypeStruct((t, d), F32)
    return pl.pallas_call(
        functools.partial(_hgrn_in_kernel, layer=layer),
        grid=(t // tm,),
        in_specs=[row,
                  _resident((1, d), lambda i: (0, 0)),
                  _resident((n_slots, d), lambda i: (0, 0)),
                  _resident((None, d, 4 * d), lambda i: (layer, 0, 0))],
        out_specs=[row, row, row, row],
        out_shape=[out, out, out, out],
        compiler_params=_params("parallel"),
        name="hgrn_in",
    )(x, g, a_lb, w_in)


def _hgrn_scan_kernel(*refs, chunk, n_sub, has_init):
    if has_init:
        q_ref, fg_ref, v_ref, og_ref, gon_ref, s0_ref, o_ref, sout_ref, st_ref = refs
    else:
        q_ref, fg_ref, v_ref, og_ref, gon_ref, o_ref, sout_ref, st_ref = refs
        s0_ref = None
    n_heads = st_ref.shape[0]
    hd = HGRN_HEAD_DIM
    c = pl.program_id(1)

    @pl.when(c == 0)
    def _():
        if has_init:
            for h in range(n_heads):
                st_ref[h] = s0_ref[0, h].T
        else:
            st_ref[...] = jnp.zeros(st_ref.shape, F32)

    ri = lax.broadcasted_iota(jnp.int32, (chunk, chunk), 0)
    ci = lax.broadcasted_iota(jnp.int32, (chunk, chunk), 1)
    causal = ri >= ci
    tri = jnp.where(causal, 1.0, 0.0).astype(BF16)
    gon = gon_ref[...]

    for s in range(n_sub):
        rows = slice(s * chunk, (s + 1) * chunk)
        fg = fg_ref[0, rows, :]
        lf = jnp.log(fg)
        kk = 1.0 - fg
        hi, mid, lo = _split3(lf)
        gcum = _dot(tri, hi) + _dot(tri, mid) + _dot(tri, lo)
        g_last = gcum[chunk - 1:chunk, :]
        q_dec = (q_ref[0, rows, :] * jnp.exp(gcum)).astype(BF16)
        k_inv = (kk * jnp.exp(-gcum)).astype(BF16)
        k_dec = (kk * jnp.exp(g_last - gcum)).astype(BF16)
        s_dec = jnp.exp(g_last)
        v = v_ref[0, rows, :].astype(BF16)
        og = og_ref[0, rows, :]

        def first_products(h, _, rows=rows, q_dec=q_dec, k_inv=k_inv, k_dec=k_dec, s_dec=s_dec, v=v):
            hs = slice(h * hd, (h + 1) * hd)
            st = st_ref[h]
            att = _dot_nt(q_dec[:, hs], k_inv[:, hs])
            inter = _dot_nt(q_dec[:, hs], st.astype(BF16))
            st_ref[h] = st * s_dec[:, hs] + _dot_tn(v[:, hs], k_dec[:, hs])
            return att, inter

        def outputs(h, prods, rows=rows, v=v, og=og):
            hs = slice(h * hd, (h + 1) * hd)
            att, inter = prods
            o = _dot(jnp.where(causal, att, 0.0).astype(BF16), v[:, hs]) + inter
            on = o * lax.rsqrt(jnp.mean(o * o, axis=-1, keepdims=True) + EPS)
            o_ref[0, rows, hs] = (on * gon[:, hs] * og[:, hs]).astype(o_ref.dtype)

        _emit_skewed(n_heads, [first_products, outputs])

    @pl.when(c == pl.num_programs(1) - 1)
    def _():
        for h in range(n_heads):
            sout_ref[0, h] = st_ref[h].T


def _hgrn_scan(q, fg, v, og, g_onorm, s0, chunk, n_sub, out_dtype):
    b, l, d = q.shape
    n_heads = d // HGRN_HEAD_DIM
    rows = chunk * n_sub
    blk = pl.BlockSpec((1, rows, d), lambda i, c: (i, c, 0))
    st_blk = pl.BlockSpec((1, n_heads, HGRN_HEAD_DIM, HGRN_HEAD_DIM), lambda i, c: (i, 0, 0, 0))
    in_specs = [blk, blk, blk, blk, _resident((1, d), lambda i, c: (0, 0))]
    args = [q, fg, v, og, g_onorm]
    if s0 is not None:
        in_specs.append(st_blk)
        args.append(s0)
    return pl.pallas_call(
        functools.partial(_hgrn_scan_kernel, chunk=chunk, n_sub=n_sub, has_init=s0 is not None),
        grid=(b, l // rows),
        in_specs=in_specs,
        out_specs=[blk, st_blk],
        out_shape=[jax.ShapeDtypeStruct((b, l, d), out_dtype),
                   jax.ShapeDtypeStruct((b, n_heads, HGRN_HEAD_DIM, HGRN_HEAD_DIM), F32)],
        scratch_shapes=[pltpu.VMEM((n_heads, HGRN_HEAD_DIM, HGRN_HEAD_DIM), F32)],
        compiler_params=_params("parallel", "arbitrary"),
        name="hgrn_scan",
    )(*args)


def _layer_tail_kernel(*refs, d_ff, ff_chunk, final):
    (h_ref, a_ref, p_ref, wo_ref, gffn_ref, win_ref, wout_ref, gple_ref, wpe_ref, wpg_ref) = refs[:10]
    h1 = h_ref[...] + _dot(a_ref[...].astype(BF16), wo_ref[...])
    hn = _rms(h1, gffn_ref[...]).astype(BF16)
    h2 = h1
    for c0 in range(0, d_ff, ff_chunk):
        gate = _dot(hn, win_ref[:, c0:c0 + ff_chunk])
        up = _dot(hn, win_ref[:, d_ff + c0:d_ff + c0 + ff_chunk])
        act = (gate * _sigmoid(gate) * up).astype(BF16)
        h2 = h2 + _dot(act, wout_ref[c0:c0 + ff_chunk, :])
    pgate = _sigmoid(_dot(_rms(h2, gple_ref[...]).astype(BF16), wpg_ref[...]))
    h3 = h2 + _dot(p_ref[...].astype(BF16), wpe_ref[...]) * pgate
    if final:
        gfin_ref, y_ref = refs[10:]
        y_ref[...] = _rms(h3, gfin_ref[...])
    else:
        gkv_ref, wkv_ref, gq_ref, wq_ref, h_out_ref, k_ref, v_ref, q_ref = refs[10:18]
        d = h3.shape[1]
        h_out_ref[...] = h3
        hkv = _rms(h3, gkv_ref[...]).astype(BF16)
        if len(refs) > 18:
            kb_ref, vb_ref = refs[18:]
            kt = _dot_nt(wkv_ref[0:d, :], hkv)
            vt = _dot_nt(wkv_ref[d:2 * d, :], hkv)
            k_ref[0] = kt
            v_ref[0] = vt
            kb_ref[0, 0] = kt.astype(BF16)
            vb_ref[0, 0] = vt.astype(BF16)
        else:
            k_ref[...] = _dot(hkv, wkv_ref[:, 0:d])
            v_ref[...] = _dot(hkv, wkv_ref[:, d:2 * d])
        hq = _rms(h3, gq_ref[...]).astype(BF16)
        q_ref[...] = (_dot(hq, wq_ref[...]) * (LOG2E * SB_HEAD_DIM ** -0.5)).astype(q_ref.dtype)


def _layer_tail(h, a, p, layer, w, tm, final, q_dtype=BF16, seq_len=None):
    t, d = h.shape
    ple = p.shape[-1]
    d_ff = w["w_ffn_out"].shape[1]
    row = pl.BlockSpec((tm, d), lambda i: (i, 0))
    vec = _resident((1, d), lambda i: (0, 0))

    def stacked(arr, idx):
        return _resident((None,) + arr.shape[1:], lambda i: (idx,) + (0,) * (arr.ndim - 1))

    j = layer - (w["w_ffn_in"].shape[0] - w["w_b_o"].shape[0])
    w_o = w["w_b_o"] if final else w["w_a_o"]
    in_specs = [row, row, pl.BlockSpec((None, tm, ple), lambda i: (layer, i, 0)),
                stacked(w_o, j if final else layer), vec, stacked(w["w_ffn_in"], layer),
                stacked(w["w_ffn_out"], layer), vec, stacked(w["w_ple_in"], layer), stacked(w["w_ple_gate"], layer)]
    args = [h, a, p, w_o, w["g_ffn"][layer][None], w["w_ffn_in"], w["w_ffn_out"], w["g_ple"][layer][None],
            w["w_ple_in"], w["w_ple_gate"]]
    if final:
        in_specs += [vec]
        args += [w["g_final"][None]]
        out_specs = [row]
        out_shape = [jax.ShapeDtypeStruct((t, d), F32)]
    else:
        w_kv = w["w_kv"] if seq_len is None else w["w_kv_t"]
        in_specs += [vec, _resident(w_kv.shape, lambda i: (0, 0)), vec, stacked(w["w_b_q"], 0)]
        args += [w["g_kv"][None], w_kv, w["g_mix"][layer + 1][None], w["w_b_q"]]
        if seq_len is None:
            out_specs = [row, row, row, row]
            out_shape = [jax.ShapeDtypeStruct((t, d), F32)] * 3 + [jax.ShapeDtypeStruct((t, d), q_dtype)]
        else:
            nb, nc = t // seq_len, seq_len // tm
            kv_t = pl.BlockSpec((1, d, tm), lambda i: (i // nc, 0, i % nc))
            kv_blocks = pl.BlockSpec((1, 1, d, tm), lambda i: (i // nc, i % nc, 0, 0))
            out_specs = [row, kv_t, kv_t, row, kv_blocks, kv_blocks]
            out_shape = ([jax.ShapeDtypeStruct((t, d), F32)] + [jax.ShapeDtypeStruct((nb, d, seq_len), F32)] * 2
                         + [jax.ShapeDtypeStruct((t, d), q_dtype)]
                         + [jax.ShapeDtypeStruct((nb, nc, d, tm), BF16)] * 2)
    return pl.pallas_call(
        functools.partial(_layer_tail_kernel, d_ff=d_ff, ff_chunk=d_ff // 2, final=final),
        grid=(t // tm,),
        in_specs=in_specs,
        out_specs=out_specs,
        out_shape=out_shape,
        compiler_params=_params("parallel"),
        name="layer_tail_final" if final else "layer_tail_kvq",
    )(*args)


def _strict_upper(n):
    ji = lax.broadcasted_iota(jnp.int32, (n, n), 0)
    si = lax.broadcasted_iota(jnp.int32, (n, n), 1)
    return jnp.where(ji > si, 1.0, 0.0).astype(BF16)


def _sb_logs(z, visible):
    m = jnp.minimum(z, 0.0)
    d = m - z
    l1p = jnp.log2(1.0 + jnp.exp2(m + d))
    log_1mb = d - l1p
    if visible is not None:
        log_1mb = jnp.where(visible, log_1mb, 0.0)
    return m - l1p, log_1mb


def _sb_weights(log_b, log_1mb, tail, carry, visible):
    carry_b = jnp.concatenate([carry] * (log_b.shape[1] // LANES), axis=1)
    a = jnp.exp2(log_b + tail + carry_b)
    if visible is not None:
        a = jnp.where(visible, a, 0.0)
    return a, carry + (tail[:, 0:1] + log_1mb[:, 0:1])


def _sb_block(z, upper, carry, visible):
    log_b, log_1mb = _sb_logs(z, visible)
    tail = _dot(log_1mb.astype(BF16), upper)
    return _sb_weights(log_b, log_1mb, tail, carry, visible)


def _emit_skewed(n_items, stages):
    state = [None] * n_items
    for t in range(n_items + len(stages) - 1):
        for s in reversed(range(len(stages))):
            i = t - s
            if 0 <= i < n_items:
                state[i] = stages[s](i, state[i])


def _sb_prompt_kernel(bias_ref, q_ref, k_ref, v_ref, o_ref, qh_ref, acc_ref, car_ref, *, blk, n_heads):
    hg = pl.program_id(1)
    qi = pl.program_id(2)

    lane = lax.broadcasted_iota(jnp.int32, (blk, LANES), 1)
    first = lane < SB_HEAD_DIM
    for j in range(n_heads):
        qt = q_ref[0, :, (j // 2) * LANES:(j // 2 + 1) * LANES]
        qh_ref[j] = jnp.where(first if j % 2 == 0 else ~first, qt, jnp.zeros_like(qt))
    upper = _strict_upper(blk)
    ti = lax.broadcasted_iota(jnp.int32, (blk, blk), 0)
    si = lax.broadcasted_iota(jnp.int32, (blk, blk), 1)
    strictly_before = si < ti
    acc_ref[...] = jnp.zeros(acc_ref.shape, F32)
    car_ref[...] = jnp.zeros(car_ref.shape, F32)

    def process(kb, visible):

        def tile(j):
            return slice((j // 2) * LANES, (j // 2 + 1) * LANES)

        def logs(j, _):
            z = _dot(qh_ref[j], k_ref[0, kb, tile(j), :]) + bias_ref[n_heads * hg + j]
            return _sb_logs(z, visible)

        def tails(j, st):
            return st + (_dot(st[1].astype(BF16), upper),)

        def weights(j, st):
            a, car_ref[j] = _sb_weights(*st, car_ref[j], visible)
            acc_ref[j] += _dot_nt(a.astype(BF16), v_ref[0, kb, tile(j), :])

        _emit_skewed(n_heads, [logs, tails, weights])

    process(qi, strictly_before)

    def body(i, carry):
        process(qi - 1 - i, None)
        return carry

    lax.fori_loop(0, qi, body, 0)
    for t in range(n_heads // 2):
        o_ref[0, :, t * LANES:(t + 1) * LANES] = jnp.where(first, acc_ref[2 * t], acc_ref[2 * t + 1]).astype(o_ref.dtype)


def _sb_prompt(q, k_t, v_t, bias, blk, n_heads):
    b, l, d = q.shape
    width = n_heads * SB_HEAD_DIM
    q_blk = pl.BlockSpec((1, blk, width), lambda i, hg, qi, bias: (i, qi, hg))
    kv_blk = pl.BlockSpec((1, l // blk, width, blk), lambda i, hg, qi, bias: (i, 0, hg, 0))
    return pl.pallas_call(
        functools.partial(_sb_prompt_kernel, blk=blk, n_heads=n_heads),
        grid_spec=pltpu.PrefetchScalarGridSpec(
            num_scalar_prefetch=1,
            grid=(b, d // width, l // blk),
            in_specs=[q_blk, kv_blk, kv_blk],
            out_specs=q_blk,
            scratch_shapes=[pltpu.VMEM((n_heads, blk, LANES), BF16),
                            pltpu.VMEM((n_heads, blk, LANES), F32), pltpu.VMEM((n_heads, blk, LANES), F32)]),
        out_shape=jax.ShapeDtypeStruct((b, l, d), BF16),
        compiler_params=_params("parallel", "parallel", "arbitrary"),
        name="sb_prompt",
    )(bias, q, k_t, v_t)


def _sb_sample_kernel(pt_ref, q_ref, kn_ref, vn_ref, bias_ref, *refs, n_step_pages, page, n_heads):
    k_refs = refs[:n_step_pages]
    v_refs = refs[n_step_pages:2 * n_step_pages]
    o_ref, qbd_ref, kn_buf, vn_buf, acc_ref, car_ref = refs[2 * n_step_pages:]
    del pt_ref
    step = pl.program_id(1)
    n_q, d = q_ref.shape[1], q_ref.shape[2]
    rows = n_q * n_heads
    upper = _strict_upper(page)
    bias = bias_ref[...]
    hrow = lax.broadcasted_iota(jnp.int32, (n_heads, d), 0)
    hlane = lax.broadcasted_iota(jnp.int32, (n_heads, d), 1) // SB_HEAD_DIM
    own_head = hrow == hlane

    def run(score_fns, value_fns, visible):
        def logs(i, _):
            return _sb_logs(score_fns[i](qbd_ref[...]) + bias, visible)

        def tails(i, st):
            return st + (_dot(st[1].astype(BF16), upper),)

        def weights(i, st):
            a, car_ref[...] = _sb_weights(*st, car_ref[...], visible)
            acc_ref[...] += value_fns[i](a.astype(BF16))

        _emit_skewed(len(score_fns), [logs, tails, weights])

    @pl.when(step == 0)
    def _():
        q = q_ref[0]
        for qi in range(n_q):
            qrow = jnp.broadcast_to(q[qi:qi + 1, :], (n_heads, d))
            qbd_ref[qi * n_heads:(qi + 1) * n_heads, :] = jnp.where(own_head, qrow, 0.0).astype(BF16)
        acc_ref[...] = jnp.zeros(acc_ref.shape, F32)
        car_ref[...] = jnp.zeros(car_ref.shape, F32)
        kn_buf[...] = jnp.zeros(kn_buf.shape, F32)
        vn_buf[...] = jnp.zeros(vn_buf.shape, F32)
        kn_buf[0:n_q, :] = kn_ref[0]
        vn_buf[0:n_q, :] = vn_ref[0]
        r_q = lax.broadcasted_iota(jnp.int32, (rows, page), 0) // n_heads
        s_k = lax.broadcasted_iota(jnp.int32, (rows, page), 1)
        run([lambda qbd: _dot_nt(qbd, kn_buf[...].astype(BF16))],
            [lambda a: _dot(a, vn_buf[...].astype(BF16))], s_k < r_q)

    run([lambda qbd, r=r: _dot(qbd, r[0].astype(BF16)) for r in k_refs],
        [lambda a, r=r: _dot_nt(a, r[0].astype(BF16)) for r in v_refs], None)

    @pl.when(step == pl.num_programs(1) - 1)
    def _():
        out_rows = []
        for qi in range(n_q):
            blk = acc_ref[qi * n_heads:(qi + 1) * n_heads, :]
            out_rows.append(jnp.sum(jnp.where(own_head, blk, 0.0), axis=0, keepdims=True))
        o_ref[0] = jnp.concatenate(out_rows, axis=0)


def _sb_sample(q, k_new, v_new, bias_rows, cache_k_t, cache_v_t, page_table, n_step_pages):
    b, n_q, d = q.shape
    n_pages = page_table.shape[1]
    page = cache_k_t.shape[2]
    n_heads = d // SB_HEAD_DIM
    rows = n_q * n_heads
    n_steps = n_pages // n_step_pages
    tok = pl.BlockSpec((1, n_q, d), lambda i, s, pt: (i, 0, 0))

    def page_spec(j):
        return pl.BlockSpec((1, d, page),
                            lambda i, s, pt: (pt[i * n_pages + (n_pages - 1 - (s * n_step_pages + j))], 0, 0))

    pages = [page_spec(j) for j in range(n_step_pages)]
    return pl.pallas_call(
        functools.partial(_sb_sample_kernel, n_step_pages=n_step_pages, page=page, n_heads=n_heads),
        grid_spec=pltpu.PrefetchScalarGridSpec(
            num_scalar_prefetch=1,
            grid=(b, n_steps),
            in_specs=[tok, tok, tok, _resident(bias_rows.shape, lambda i, s, pt: (0, 0))] + pages + pages,
            out_specs=tok,
            scratch_shapes=[pltpu.VMEM((rows, d), BF16), pltpu.VMEM((page, d), F32), pltpu.VMEM((page, d), F32),
                            pltpu.VMEM((rows, d), F32), pltpu.VMEM((rows, LANES), F32)]),
        out_shape=jax.ShapeDtypeStruct((b, n_q, d), F32),
        compiler_params=_params("parallel", "arbitrary"),
        name="sb_sample",
    )(page_table.reshape(-1), q, k_new, v_new, bias_rows,
      *([cache_k_t] * n_step_pages), *([cache_v_t] * n_step_pages))


HGRN_CHUNK = 64
HGRN_SUBCHUNKS = 4
SAMPLE_PAD = 8
PROMPT_BLOCK = 256
PROMPT_HEADS = 16
STEP_PAGES = 8


def _row_tile(t, want):
    return want if t % want == 0 else t


def _trunk(x, p, state0, past, w):
    b, l, d = x.shape
    t = b * l
    x2 = x.reshape(t, d)
    p2 = p.reshape(p.shape[0], t, p.shape[-1])
    q, fg, v, og = _hgrn_in(x2, w["g_mix"][0][None], w["a_lb"], w["w_a_in"], 0, _row_tile(t, 512))
    q, fg, v, og = (a.reshape(b, l, d) for a in (q, fg, v, og))
    if l % (HGRN_CHUNK * HGRN_SUBCHUNKS) == 0:
        o, s_fin = _hgrn_scan(q, fg, v, og, w["g_a_onorm"][0][None], state0, HGRN_CHUNK, HGRN_SUBCHUNKS, BF16)
    else:
        pad = ((0, 0), (0, SAMPLE_PAD - l), (0, 0))
        o, s_fin = _hgrn_scan(jnp.pad(q, pad), jnp.pad(fg, pad, constant_values=1.0), jnp.pad(v, pad),
                              jnp.pad(og, pad), w["g_a_onorm"][0][None], state0, SAMPLE_PAD, 1, F32)
        o = o[:, :l]
    n_heads = d // SB_HEAD_DIM
    if past is None:
        tm = PROMPT_BLOCK
        h, k_t, v_t, q1, kb_t, vb_t = _layer_tail(x2, o.reshape(t, d), p2, 0, w, tm, final=False, seq_len=l)
        attn = _sb_prompt(q1.reshape(b, l, d), kb_t, vb_t, w["sb_bias"][0], PROMPT_BLOCK, PROMPT_HEADS)
        k, v_kv = (jnp.transpose(a.reshape(b, n_heads, SB_HEAD_DIM, l), (0, 3, 1, 2)) for a in (k_t, v_t))
    else:
        tm = _row_tile(t, 256)
        h, k, v_kv, q1 = _layer_tail(x2, o.reshape(t, d), p2, 0, w, tm, final=False, q_dtype=F32)
        cache_k, cache_v, page_table = past
        bias_rows = jnp.broadcast_to(jnp.tile(w["sb_bias"][0], l)[:, None], (l * n_heads, LANES))
        cache_k_t, cache_v_t = (jnp.transpose(c, (0, 2, 3, 1)).reshape(c.shape[0], d, c.shape[1])
                                for c in (cache_k, cache_v))
        attn = _sb_sample(q1.reshape(b, l, d), k.reshape(b, l, d), v_kv.reshape(b, l, d), bias_rows,
                          cache_k_t, cache_v_t, page_table, min(STEP_PAGES, page_table.shape[1]))
        k, v_kv = (a.reshape(b, l, n_heads, SB_HEAD_DIM) for a in (k, v_kv))
    (y,) = _layer_tail(h, attn.reshape(t, d), p2, 1, w, tm, final=True)
    return y.reshape(b, l, d), s_fin[None], k, v_kv


def kernel(x_prompt, x_sample, p_prompt, p_sample, state_hgrn, cache_k, cache_v, page_table, a_lb, w_a_in,
           g_a_onorm, w_a_o, g_kv, w_kv, w_b_q, w_b_o, sb_bias, g_mix, g_ffn, w_ffn_in, w_ffn_out, g_ple,
           w_ple_in, w_ple_gate, g_final):
    w = dict(a_lb=a_lb, g_a_onorm=g_a_onorm, g_kv=g_kv, sb_bias=sb_bias * LOG2E, g_mix=g_mix, g_ffn=g_ffn, g_ple=g_ple,
             g_final=g_final,
             w_a_in=w_a_in.astype(BF16), w_a_o=w_a_o.astype(BF16), w_kv=w_kv.astype(BF16),
             w_kv_t=w_kv.T.astype(BF16),
             w_b_q=w_b_q.astype(BF16), w_b_o=w_b_o.astype(BF16), w_ffn_in=w_ffn_in.astype(BF16),
             w_ffn_out=w_ffn_out.astype(BF16), w_ple_in=w_ple_in.astype(BF16),
             w_ple_gate=w_ple_gate.astype(BF16))
    y_p, st_p, k_p, v_p = _trunk(x_prompt, p_prompt, None, None, w)
    y_s, st_s, k_s, v_s = _trunk(x_sample, p_sample, state_hgrn[0], (cache_k, cache_v, page_table), w)
    return (y_p, y_s, st_p, st_s, k_p, v_p, k_s, v_s)
```

```python
import functools

import jax
import jax.numpy as jnp
from jax import lax
from jax.experimental import pallas as pl
from jax.experimental.pallas import tpu as pltpu

F32 = jnp.float32
BF16 = jnp.bfloat16
EPS = 1e-6
LOG2E = 1.4426950408889634
HGRN_HEAD_DIM = 128
SB_HEAD_DIM = 64
LANES = 128
VMEM_LIMIT_BYTES = 56 * 1024 * 1024


def _sigmoid(x):
    return 1.0 / (1.0 + jnp.exp(-x))


def _rms(x, g):
    return x * lax.rsqrt(jnp.mean(x * x, axis=-1, keepdims=True) + EPS) * g


def _dot(a, b):
    return jnp.dot(a, b, preferred_element_type=F32)


def _dot_nt(a, b):
    return lax.dot_general(a, b, (((1,), (1,)), ((), ())), preferred_element_type=F32)


def _dot_tn(a, b):
    return lax.dot_general(a, b, (((0,), (0,)), ((), ())), preferred_element_type=F32)


def _split3(x):
    hi = x.astype(BF16)
    r = x - hi.astype(F32)
    mid = r.astype(BF16)
    lo = (r - mid.astype(F32)).astype(BF16)
    return hi, mid, lo


def _params(*sem):
    return pltpu.CompilerParams(dimension_semantics=sem, vmem_limit_bytes=VMEM_LIMIT_BYTES)


def _resident(shape, index_map):
    return pl.BlockSpec(shape, index_map, pipeline_mode=pl.Buffered(1))


def _hgrn_in_kernel(x_ref, g_ref, alb_ref, w_ref, q_ref, fg_ref, v_ref, og_ref, *, layer):
    d = x_ref.shape[1]
    hn = _rms(x_ref[...], g_ref[...]).astype(BF16)
    a = alb_ref[...]
    e = jnp.exp(a - jnp.max(a, axis=0, keepdims=True))
    lb = jnp.sum(e[0:layer + 1], axis=0, keepdims=True) / jnp.sum(e, axis=0, keepdims=True)
    pq = _dot(hn, w_ref[:, 0:d])
    q_ref[...] = pq * _sigmoid(pq)
    pf = _dot(hn, w_ref[:, d:2 * d])
    fg_ref[...] = lb + (1.0 - lb) * _sigmoid(pf)
    v_ref[...] = _dot(hn, w_ref[:, 2 * d:3 * d])
    po = _dot(hn, w_ref[:, 3 * d:4 * d])
    og_ref[...] = po * _sigmoid(po)


def _hgrn_in(x, g, a_lb, w_in, layer, tm):
    t, d = x.shape
    n_slots = a_lb.shape[0]
    row = pl.BlockSpec((tm, d), lambda i: (i, 0))
    out = jax.ShapeDtypeStruct((t, d), F32)
    return pl.pallas_call(
        functools.partial(_hgrn_in_kernel, layer=layer),
        grid=(t // tm,),
        in_specs=[row,
                  _resident((1, d), lambda i: (0, 0)),
                  _resident((n_slots, d), lambda i: (0, 0)),
                  _resident((None, d, 4 * d), lambda i: (layer, 0, 0))],
        out_specs=[row, row, row, row],
        out_shape=[out, out, out, out],
        compiler_params=_params("parallel"),
        name="hgrn_in",
    )(x, g, a_lb, w_in)


def _hgrn_chunk(refs, e, rows, size, exact):
    q_ref, fg_ref, v_ref, og_ref, gon_ref, o_ref, st_ref = refs
    hd = HGRN_HEAD_DIM
    ri = lax.broadcasted_iota(jnp.int32, (size, size), 0)
    ci = lax.broadcasted_iota(jnp.int32, (size, size), 1)
    causal = ri >= ci
    tri = jnp.where(causal, 1.0, 0.0).astype(BF16)
    fg = fg_ref[e, rows, :]
    kk = 1.0 - fg
    hi, mid, lo = _split3(jnp.log(fg))
    gcum = _dot(tri, hi) + _dot(tri, mid) + _dot(tri, lo)
    g_last = gcum[size - 1:size, :]
    q = q_ref[e, rows, :]
    q_dec = (q * jnp.exp(gcum)).astype(BF16)
    k_dec = (kk * jnp.exp(g_last - gcum)).astype(BF16)
    s_dec = jnp.exp(g_last)
    v = v_ref[e, rows, :].astype(BF16)
    og = og_ref[e, rows, :]
    gon = gon_ref[...]
    if exact:
        row = lax.broadcasted_iota(jnp.int32, gcum.shape, 0)
        pair_terms = [q * jnp.exp(jnp.where(row >= s, gcum - gcum[s:s + 1], 0.0)) * kk[s:s + 1] for s in range(size)]
    else:
        k_inv = (kk * jnp.exp(-gcum)).astype(BF16)

    def first_products(h, _):
        hs = slice(h * hd, (h + 1) * hd)
        st = st_ref[e, h]
        if exact:
            att = jnp.zeros((size, size), F32)
            for s, t in enumerate(pair_terms):
                att = att + jnp.where(ci == s, jnp.sum(t[:, hs], axis=-1, keepdims=True), 0.0)
        else:
            att = _dot_nt(q_dec[:, hs], k_inv[:, hs])
        inter = _dot_nt(q_dec[:, hs], st.astype(BF16))
        st_ref[e, h] = st * s_dec[:, hs] + _dot_tn(v[:, hs], k_dec[:, hs])
        return att, inter

    def outputs(h, prods):
        hs = slice(h * hd, (h + 1) * hd)
        att, inter = prods
        o = _dot(jnp.where(causal, att, 0.0).astype(BF16), v[:, hs]) + inter
        on = o * lax.rsqrt(jnp.mean(o * o, axis=-1, keepdims=True) + EPS)
        o_ref[e, rows, hs] = (on * gon[:, hs] * og[:, hs]).astype(o_ref.dtype)

    return (first_products, outputs), g_last


def _hgrn_scan_kernel(*refs, chunk, n_sub, has_init):
    if has_init:
        q_ref, fg_ref, v_ref, og_ref, gon_ref, s0_ref, o_ref, sout_ref, st_ref, st0_ref = refs
    else:
        q_ref, fg_ref, v_ref, og_ref, gon_ref, o_ref, sout_ref, st_ref, st0_ref = refs
        s0_ref = None
    n_elems, n_heads = st_ref.shape[0], st_ref.shape[1]
    c = pl.program_id(1)
    chunk_refs = (q_ref, fg_ref, v_ref, og_ref, gon_ref, o_ref, st_ref)

    @pl.when(c == 0)
    def _():
        if has_init:
            for e in range(n_elems):
                for h in range(n_heads):
                    st_ref[e, h] = s0_ref[e, h].T
        else:
            st_ref[...] = jnp.zeros(st_ref.shape, F32)

    st0_ref[...] = st_ref[...]
    stages, worst = [], None
    for e in range(n_elems):
        for s in range(n_sub):
            fns, g_last = _hgrn_chunk(chunk_refs, e, slice(s * chunk, (s + 1) * chunk), chunk, exact=False)
            worst = g_last if worst is None else jnp.minimum(worst, g_last)
            stages += [[functools.partial(fn, h) for fn in fns] for h in range(n_heads)]
    _emit_skewed(len(stages), [lambda i, st, k=k: stages[i][k](st) for k in range(2)], lag=HGRN_STAGE_LAG)

    @pl.when(jnp.min(worst) < -HGRN_DECAY_LIMIT)
    def _():
        st_ref[...] = st0_ref[...]
        small = min(HGRN_EXACT_CHUNK, chunk * n_sub)

        def redo(i, carry):
            rows = pl.ds(pl.multiple_of(i * small, small), small)
            for e in range(n_elems):
                fns, _ = _hgrn_chunk(chunk_refs, e, rows, small, exact=True)
                for h in range(n_heads):
                    fns[1](h, fns[0](h, None))
            return carry

        lax.fori_loop(0, chunk * n_sub // small, redo, 0)

    @pl.when(c == pl.num_programs(1) - 1)
    def _():
        for e in range(n_elems):
            for h in range(n_heads):
                sout_ref[e, h] = st_ref[e, h].T


def _hgrn_scan(q, fg, v, og, g_onorm, s0, chunk, n_sub, n_elems, out_dtype):
    b, l, d = q.shape
    n_heads = d // HGRN_HEAD_DIM
    rows = chunk * n_sub
    blk = pl.BlockSpec((n_elems, rows, d), lambda i, c: (i, c, 0))
    st_shape = (n_elems, n_heads, HGRN_HEAD_DIM, HGRN_HEAD_DIM)
    st_blk = pl.BlockSpec(st_shape, lambda i, c: (i, 0, 0, 0))
    in_specs = [blk, blk, blk, blk, _resident((1, d), lambda i, c: (0, 0))]
    args = [q, fg, v, og, g_onorm]
    if s0 is not None:
        in_specs.append(st_blk)
        args.append(s0)
    return pl.pallas_call(
        functools.partial(_hgrn_scan_kernel, chunk=chunk, n_sub=n_sub, has_init=s0 is not None),
        grid=(b // n_elems, l // rows),
        in_specs=in_specs,
        out_specs=[blk, st_blk],
        out_shape=[jax.ShapeDtypeStruct((b, l, d), out_dtype),
                   jax.ShapeDtypeStruct((b, n_heads, HGRN_HEAD_DIM, HGRN_HEAD_DIM), F32)],
        scratch_shapes=[pltpu.VMEM(st_shape, F32), pltpu.VMEM(st_shape, F32)],
        compiler_params=_params("parallel", "arbitrary"),
        name="hgrn_scan",
    )(*args)


def _layer_tail_kernel(*refs, d_ff, ff_chunk, final):
    (h_ref, a_ref, p_ref, wo_ref, gffn_ref, win_ref, wout_ref, gple_ref, wpe_ref, wpg_ref) = refs[:10]
    h1 = h_ref[...] + _dot(a_ref[...].astype(BF16), wo_ref[...])
    hn = _rms(h1, gffn_ref[...]).astype(BF16)
    h2 = h1
    for c0 in range(0, d_ff, ff_chunk):
        gate = _dot(hn, win_ref[:, c0:c0 + ff_chunk])
        up = _dot(hn, win_ref[:, d_ff + c0:d_ff + c0 + ff_chunk])
        act = (gate * _sigmoid(gate) * up).astype(BF16)
        h2 = h2 + _dot(act, wout_ref[c0:c0 + ff_chunk, :])
    pgate = _sigmoid(_dot(_rms(h2, gple_ref[...]).astype(BF16), wpg_ref[...]))
    h3 = h2 + _dot(p_ref[...].astype(BF16), wpe_ref[...]) * pgate
    if final:
        gfin_ref, y_ref = refs[10:]
        y_ref[...] = _rms(h3, gfin_ref[...])
    else:
        gkv_ref, wkv_ref, gq_ref, wq_ref, h_out_ref, k_ref, v_ref, q_ref = refs[10:18]
        d = h3.shape[1]
        h_out_ref[...] = h3
        hkv = _rms(h3, gkv_ref[...]).astype(BF16)
        if len(refs) > 18:
            kb_ref, vb_ref = refs[18:]
            kt = _dot_nt(wkv_ref[0:d, :], hkv)
            vt = _dot_nt(wkv_ref[d:2 * d, :], hkv)
            k_ref[0] = kt
            v_ref[0] = vt
            kb_ref[0, 0] = kt.astype(BF16)
            vb_ref[0, 0] = vt.astype(BF16)
        else:
            k_ref[...] = _dot(hkv, wkv_ref[:, 0:d])
            v_ref[...] = _dot(hkv, wkv_ref[:, d:2 * d])
        hq = _rms(h3, gq_ref[...]).astype(BF16)
        q_ref[...] = (_dot(hq, wq_ref[...]) * (LOG2E * SB_HEAD_DIM ** -0.5)).astype(q_ref.dtype)


def _layer_tail(h, a, p, layer, w, tm, final, q_dtype=BF16, seq_len=None):
    t, d = h.shape
    ple = p.shape[-1]
    d_ff = w["w_ffn_out"].shape[1]
    row = pl.BlockSpec((tm, d), lambda i: (i, 0))
    vec = _resident((1, d), lambda i: (0, 0))

    def stacked(arr, idx):
        return _resident((None,) + arr.shape[1:], lambda i: (idx,) + (0,) * (arr.ndim - 1))

    j = layer - (w["w_ffn_in"].shape[0] - w["w_b_o"].shape[0])
    w_o = w["w_b_o"] if final else w["w_a_o"]
    in_specs = [row, row, pl.BlockSpec((None, tm, ple), lambda i: (layer, i, 0)),
                stacked(w_o, j if final else layer), vec, stacked(w["w_ffn_in"], layer),
                stacked(w["w_ffn_out"], layer), vec, stacked(w["w_ple_in"], layer), stacked(w["w_ple_gate"], layer)]
    args = [h, a, p, w_o, w["g_ffn"][layer][None], w["w_ffn_in"], w["w_ffn_out"], w["g_ple"][layer][None],
            w["w_ple_in"], w["w_ple_gate"]]
    if final:
        in_specs += [vec]
        args += [w["g_final"][None]]
        out_specs = [row]
        out_shape = [jax.ShapeDtypeStruct((t, d), F32)]
    else:
        w_kv = w["w_kv"] if seq_len is None else w["w_kv_t"]
        in_specs += [vec, _resident(w_kv.shape, lambda i: (0, 0)), vec, stacked(w["w_b_q"], 0)]
        args += [w["g_kv"][None], w_kv, w["g_mix"][layer + 1][None], w["w_b_q"]]
        if seq_len is None:
            out_specs = [row, row, row, row]
            out_shape = [jax.ShapeDtypeStruct((t, d), F32)] * 3 + [jax.ShapeDtypeStruct((t, d), q_dtype)]
        else:
            nb, nc = t // seq_len, seq_len // tm
            kv_t = pl.BlockSpec((1, d, tm), lambda i: (i // nc, 0, i % nc))
            kv_blocks = pl.BlockSpec((1, 1, d, tm), lambda i: (i // nc, i % nc, 0, 0))
            out_specs = [row, kv_t, kv_t, row, kv_blocks, kv_blocks]
            out_shape = ([jax.ShapeDtypeStruct((t, d), F32)] + [jax.ShapeDtypeStruct((nb, d, seq_len), F32)] * 2
                         + [jax.ShapeDtypeStruct((t, d), q_dtype)]
                         + [jax.ShapeDtypeStruct((nb, nc, d, tm), BF16)] * 2)
    return pl.pallas_call(
        functools.partial(_layer_tail_kernel, d_ff=d_ff, ff_chunk=d_ff // 2, final=final),
        grid=(t // tm,),
        in_specs=in_specs,
        out_specs=out_specs,
        out_shape=out_shape,
        compiler_params=_params("parallel"),
        name="layer_tail_final" if final else "layer_tail_kvq",
    )(*args)


def _strict_upper(n):
    ji = lax.broadcasted_iota(jnp.int32, (n, n), 0)
    si = lax.broadcasted_iota(jnp.int32, (n, n), 1)
    return jnp.where(ji > si, 1.0, 0.0).astype(BF16)


def _sb_logs(z, visible):
    m = jnp.minimum(z, 0.0)
    d = m - z
    l1p = jnp.log2(1.0 + jnp.exp2(m + d))
    log_1mb = d - l1p
    if visible is not None:
        log_1mb = jnp.where(visible, log_1mb, 0.0)
    return m - l1p, log_1mb


def _sb_weights(log_b, log_1mb, tail, carry, visible):
    carry_b = jnp.concatenate([carry] * (log_b.shape[1] // LANES), axis=1)
    a = jnp.exp2(log_b + tail + carry_b)
    if visible is not None:
        a = jnp.where(visible, a, 0.0)
    return a, carry + (tail[:, 0:1] + log_1mb[:, 0:1])


def _sb_block(z, upper, carry, visible):
    log_b, log_1mb = _sb_logs(z, visible)
    tail = _dot(log_1mb.astype(BF16), upper)
    return _sb_weights(log_b, log_1mb, tail, carry, visible)


def _emit_skewed(n_items, stages, lag=1):
    state = [None] * n_items
    for t in range(n_items + (len(stages) - 1) * lag):
        for s in reversed(range(len(stages))):
            i = t - s * lag
            if 0 <= i < n_items:
                state[i] = stages[s](i, state[i])


def _sb_prompt_kernel(bias_ref, q_ref, k_ref, v_ref, o_ref, qh_ref, acc_ref, car_ref, *, blk, n_heads):
    hg = pl.program_id(1)
    qi = pl.program_id(2)

    lane = lax.broadcasted_iota(jnp.int32, (blk, LANES), 1)
    first = lane < SB_HEAD_DIM
    for j in range(n_heads):
        qt = q_ref[0, :, (j // 2) * LANES:(j // 2 + 1) * LANES]
        qh_ref[j] = jnp.where(first if j % 2 == 0 else ~first, qt, jnp.zeros_like(qt))
    upper = _strict_upper(blk)
    ti = lax.broadcasted_iota(jnp.int32, (blk, blk), 0)
    si = lax.broadcasted_iota(jnp.int32, (blk, blk), 1)
    strictly_before = si < ti
    acc_ref[...] = jnp.zeros(acc_ref.shape, F32)
    car_ref[...] = jnp.zeros(car_ref.shape, F32)

    def process(kb, visible):

        def tile(j):
            return slice((j // 2) * LANES, (j // 2 + 1) * LANES)

        def logs(j, _):
            z = _dot(qh_ref[j], k_ref[0, kb, tile(j), :]) + bias_ref[n_heads * hg + j]
            return _sb_logs(z, visible)

        def tails(j, st):
            return st + (_dot(st[1].astype(BF16), upper),)

        def weights(j, st):
            a, car_ref[j] = _sb_weights(*st, car_ref[j], visible)
            acc_ref[j] += _dot_nt(a.astype(BF16), v_ref[0, kb, tile(j), :])

        _emit_skewed(n_heads, [logs, tails, weights])

    process(qi, strictly_before)

    def body(i, carry):
        process(qi - 1 - i, None)
        return carry

    lax.fori_loop(0, qi, body, 0)
    for t in range(n_heads // 2):
        o_ref[0, :, t * LANES:(t + 1) * LANES] = jnp.where(first, acc_ref[2 * t], acc_ref[2 * t + 1]).astype(o_ref.dtype)


def _sb_prompt(q, k_t, v_t, bias, blk, n_heads):
    b, l, d = q.shape
    width = n_heads * SB_HEAD_DIM
    q_blk = pl.BlockSpec((1, blk, width), lambda i, hg, qi, bias: (i, qi, hg))
    kv_blk = pl.BlockSpec((1, l // blk, width, blk), lambda i, hg, qi, bias: (i, 0, hg, 0))
    return pl.pallas_call(
        functools.partial(_sb_prompt_kernel, blk=blk, n_heads=n_heads),
        grid_spec=pltpu.PrefetchScalarGridSpec(
            num_scalar_prefetch=1,
            grid=(b, d // width, l // blk),
            in_specs=[q_blk, kv_blk, kv_blk],
            out_specs=q_blk,
            scratch_shapes=[pltpu.VMEM((n_heads, blk, LANES), BF16),
                            pltpu.VMEM((n_heads, blk, LANES), F32), pltpu.VMEM((n_heads, blk, LANES), F32)]),
        out_shape=jax.ShapeDtypeStruct((b, l, d), BF16),
        compiler_params=_params("parallel", "parallel", "arbitrary"),
        name="sb_prompt",
    )(bias, q, k_t, v_t)


def _sb_sample_kernel(pt_ref, q_ref, kn_ref, vn_ref, bias_ref, *refs, n_step_pages, page, n_heads, single_step):
    k_refs = refs[:n_step_pages]
    v_refs = refs[n_step_pages:2 * n_step_pages]
    o_ref, qbd_ref, kn_buf, vn_buf, acc_ref, car_ref = refs[2 * n_step_pages:]
    del pt_ref
    step = pl.program_id(1)
    n_q, d = q_ref.shape[1], q_ref.shape[2]
    rows = n_q * n_heads
    upper = _strict_upper(page)
    bias = bias_ref[...]
    hrow = lax.broadcasted_iota(jnp.int32, (n_heads, d), 0)
    hlane = lax.broadcasted_iota(jnp.int32, (n_heads, d), 1) // SB_HEAD_DIM
    own_head = hrow == hlane

    def run(blocks):
        def logs(i, _):
            return _sb_logs(blocks[i][0](qbd_ref[...]) + bias, blocks[i][2])

        def tails(i, st):
            return st + (_dot(st[1].astype(BF16), upper),)

        def weights(i, st):
            a, car_ref[...] = _sb_weights(*st, car_ref[...], blocks[i][2])
            acc_ref[...] += blocks[i][1](a.astype(BF16))

        _emit_skewed(len(blocks), [logs, tails, weights])

    def start():
        q = q_ref[0]
        for qi in range(n_q):
            qrow = jnp.broadcast_to(q[qi:qi + 1, :], (n_heads, d))
            qbd_ref[qi * n_heads:(qi + 1) * n_heads, :] = jnp.where(own_head, qrow, 0.0).astype(BF16)
        acc_ref[...] = jnp.zeros(acc_ref.shape, F32)
        car_ref[...] = jnp.zeros(car_ref.shape, F32)
        kn_buf[...] = jnp.zeros(kn_buf.shape, F32)
        vn_buf[...] = jnp.zeros(vn_buf.shape, F32)
        kn_buf[0:n_q, :] = kn_ref[0]
        vn_buf[0:n_q, :] = vn_ref[0]

    r_q = lax.broadcasted_iota(jnp.int32, (rows, page), 0) // n_heads
    s_k = lax.broadcasted_iota(jnp.int32, (rows, page), 1)
    new_block = (lambda qbd: _dot_nt(qbd, kn_buf[...].astype(BF16)),
                 lambda a: _dot(a, vn_buf[...].astype(BF16)), s_k < r_q)
    pages = [(lambda qbd, r=kr: _dot(qbd, r[0].astype(BF16)), lambda a, r=vr: _dot_nt(a, r[0].astype(BF16)), None)
             for kr, vr in zip(k_refs, v_refs)]
    if single_step:
        start()
        run([new_block] + pages)
    else:
        @pl.when(step == 0)
        def _():
            start()
            run([new_block])

        run(pages)

    @pl.when(step == pl.num_programs(1) - 1)
    def _():
        out_rows = []
        for qi in range(n_q):
            blk = acc_ref[qi * n_heads:(qi + 1) * n_heads, :]
            out_rows.append(jnp.sum(jnp.where(own_head, blk, 0.0), axis=0, keepdims=True))
        o_ref[0] = jnp.concatenate(out_rows, axis=0)


def _sb_sample(q, k_new, v_new, bias_rows, cache_k_t, cache_v_t, page_table, n_step_pages):
    b, n_q, d = q.shape
    n_pages = page_table.shape[1]
    page = cache_k_t.shape[2]
    n_heads = d // SB_HEAD_DIM
    rows = n_q * n_heads
    n_steps = n_pages // n_step_pages
    tok = pl.BlockSpec((1, n_q, d), lambda i, s, pt: (i, 0, 0))

    def page_spec(j):
        return pl.BlockSpec((1, d, page),
                            lambda i, s, pt: (pt[i * n_pages + (n_pages - 1 - (s * n_step_pages + j))], 0, 0))

    pages = [page_spec(j) for j in range(n_step_pages)]
    return pl.pallas_call(
        functools.partial(_sb_sample_kernel, n_step_pages=n_step_pages, page=page, n_heads=n_heads,
                          single_step=n_steps == 1),
        grid_spec=pltpu.PrefetchScalarGridSpec(
            num_scalar_prefetch=1,
            grid=(b, n_steps),
            in_specs=[tok, tok, tok, _resident(bias_rows.shape, lambda i, s, pt: (0, 0))] + pages + pages,
            out_specs=tok,
            scratch_shapes=[pltpu.VMEM((rows, d), BF16), pltpu.VMEM((page, d), F32), pltpu.VMEM((page, d), F32),
                            pltpu.VMEM((rows, d), F32), pltpu.VMEM((rows, LANES), F32)]),
        out_shape=jax.ShapeDtypeStruct((b, n_q, d), F32),
        compiler_params=_params("parallel", "arbitrary"),
        name="sb_sample",
    )(page_table.reshape(-1), q, k_new, v_new, bias_rows,
      *([cache_k_t] * n_step_pages), *([cache_v_t] * n_step_pages))


HGRN_CHUNK = 64
HGRN_SUBCHUNKS = 4
HGRN_STAGE_LAG = 3
HGRN_DECAY_LIMIT = 80.0
HGRN_EXACT_CHUNK = 16
SAMPLE_PAD = 8
SAMPLE_ELEMS = 4
PROMPT_BLOCK = 256
PROMPT_HEADS = 16
STEP_PAGES = 16


def _row_tile(t, want):
    return want if t % want == 0 else t


def _trunk(x, p, state0, past, w):
    b, l, d = x.shape
    t = b * l
    x2 = x.reshape(t, d)
    p2 = p.reshape(p.shape[0], t, p.shape[-1])
    q, fg, v, og = _hgrn_in(x2, w["g_mix"][0][None], w["a_lb"], w["w_a_in"], 0, _row_tile(t, 512))
    q, fg, v, og = (a.reshape(b, l, d) for a in (q, fg, v, og))
    if l % (HGRN_CHUNK * HGRN_SUBCHUNKS) == 0:
        o, s_fin = _hgrn_scan(q, fg, v, og, w["g_a_onorm"][0][None], state0, HGRN_CHUNK, HGRN_SUBCHUNKS, 1, BF16)
    else:
        pad = ((0, 0), (0, SAMPLE_PAD - l), (0, 0))
        n_elems = SAMPLE_ELEMS if b % SAMPLE_ELEMS == 0 else 1
        o, s_fin = _hgrn_scan(jnp.pad(q, pad), jnp.pad(fg, pad, constant_values=1.0), jnp.pad(v, pad),
                              jnp.pad(og, pad), w["g_a_onorm"][0][None], state0, SAMPLE_PAD, 1, n_elems, F32)
        o = o[:, :l]
    n_heads = d // SB_HEAD_DIM
    if past is None:
        tm = PROMPT_BLOCK
        h, k_t, v_t, q1, kb_t, vb_t = _layer_tail(x2, o.reshape(t, d), p2, 0, w, tm, final=False, seq_len=l)
        attn = _sb_prompt(q1.reshape(b, l, d), kb_t, vb_t, w["sb_bias"][0], PROMPT_BLOCK, PROMPT_HEADS)
        k, v_kv = (jnp.transpose(a.reshape(b, n_heads, SB_HEAD_DIM, l), (0, 3, 1, 2)) for a in (k_t, v_t))
    else:
        tm = _row_tile(t, 256)
        h, k, v_kv, q1 = _layer_tail(x2, o.reshape(t, d), p2, 0, w, tm, final=False, q_dtype=F32)
        cache_k, cache_v, page_table = past
        bias_rows = jnp.broadcast_to(jnp.tile(w["sb_bias"][0], l)[:, None], (l * n_heads, LANES))
        cache_k_t, cache_v_t = (jnp.transpose(c, (0, 2, 3, 1)).reshape(c.shape[0], d, c.shape[1])
                                for c in (cache_k, cache_v))
        attn = _sb_sample(q1.reshape(b, l, d), k.reshape(b, l, d), v_kv.reshape(b, l, d), bias_rows,
                          cache_k_t, cache_v_t, page_table, min(STEP_PAGES, page_table.shape[1]))
        k, v_kv = (a.reshape(b, l, n_heads, SB_HEAD_DIM) for a in (k, v_kv))
    (y,) = _layer_tail(h, attn.reshape(t, d), p2, 1, w, tm, final=True)
    return y.reshape(b, l, d), s_fin[None], k, v_kv


def kernel(x_prompt, x_sample, p_prompt, p_sample, state_hgrn, cache_k, cache_v, page_table, a_lb, w_a_in,
           g_a_onorm, w_a_o, g_kv, w_kv, w_b_q, w_b_o, sb_bias, g_mix, g_ffn, w_ffn_in, w_ffn_out, g_ple,
           w_ple_in, w_ple_gate, g_final):
    w = dict(a_lb=a_lb, g_a_onorm=g_a_onorm, g_kv=g_kv, sb_bias=sb_bias * LOG2E, g_mix=g_mix, g_ffn=g_ffn, g_ple=g_ple,
             g_final=g_final,
             w_a_in=w_a_in.astype(BF16), w_a_o=w_a_o.astype(BF16), w_kv=w_kv.astype(BF16),
             w_kv_t=w_kv.T.astype(BF16),
             w_b_q=w_b_q.astype(BF16), w_b_o=w_b_o.astype(BF16), w_ffn_in=w_ffn_in.astype(BF16),
             w_ffn_out=w_ffn_out.astype(BF16), w_ple_in=w_ple_in.astype(BF16),
             w_ple_gate=w_ple_gate.astype(BF16))
    y_p, st_p, k_p, v_p = _trunk(x_prompt, p_prompt, None, None, w)
    y_s, st_s, k_s, v_s = _trunk(x_sample, p_sample, state_hgrn[0], (cache_k, cache_v, page_table), w)
    return (y_p, y_s, st_p, st_s, k_p, v_p, k_s, v_s)
```

```python
import functools

import jax
import jax.numpy as jnp
from jax import lax
from jax.experimental import pallas as pl
from jax.experimental.pallas import tpu as pltpu

F32 = jnp.float32
BF16 = jnp.bfloat16
EPS = 1e-6
LOG2E = 1.4426950408889634
HGRN_HEAD_DIM = 128
SB_HEAD_DIM = 64
LANES = 128
VMEM_LIMIT_BYTES = 56 * 1024 * 1024


def _sigmoid(x):
    return 1.0 / (1.0 + jnp.exp(-x))


def _rms(x, g):
    return x * lax.rsqrt(jnp.mean(x * x, axis=-1, keepdims=True) + EPS) * g


def _dot(a, b):
    return jnp.dot(a, b, preferred_element_type=F32)


def _dot_nt(a, b):
    return lax.dot_general(a, b, (((1,), (1,)), ((), ())), preferred_element_type=F32)


def _dot_tn(a, b):
    return lax.dot_general(a, b, (((0,), (0,)), ((), ())), preferred_element_type=F32)


def _split3(x):
    hi = x.astype(BF16)
    r = x - hi.astype(F32)
    mid = r.astype(BF16)
    lo = (r - mid.astype(F32)).astype(BF16)
    return hi, mid, lo


def _params(*sem):
    return pltpu.CompilerParams(dimension_semantics=sem, vmem_limit_bytes=VMEM_LIMIT_BYTES)


def _resident(shape, index_map):
    return pl.BlockSpec(shape, index_map, pipeline_mode=pl.Buffered(1))


def _hgrn_in_kernel(x_ref, g_ref, alb_ref, w_ref, q_ref, fg_ref, v_ref, og_ref, *, layer):
    d = x_ref.shape[1]
    hn = _rms(x_ref[...], g_ref[...]).astype(BF16)
    a = alb_ref[...]
    e = jnp.exp(a - jnp.max(a, axis=0, keepdims=True))
    lb = jnp.sum(e[0:layer + 1], axis=0, keepdims=True) / jnp.sum(e, axis=0, keepdims=True)
    pq = _dot(hn, w_ref[:, 0:d])
    q_ref[...] = pq * _sigmoid(pq)
    pf = _dot(hn, w_ref[:, d:2 * d])
    fg_ref[...] = lb + (1.0 - lb) * _sigmoid(pf)
    v_ref[...] = _dot(hn, w_ref[:, 2 * d:3 * d])
    po = _dot(hn, w_ref[:, 3 * d:4 * d])
    og_ref[...] = po * _sigmoid(po)


def _hgrn_in(x, g, a_lb, w_in, layer, tm):
    t, d = x.shape
    n_slots = a_lb.shape[0]
    row = pl.BlockSpec((tm, d), lambda i: (i, 0))
    out = jax.ShapeDtypeStruct((t, d), F32)
    return pl.pallas_call(
        functools.partial(_hgrn_in_kernel, layer=layer),
        grid=(t // tm,),
        in_specs=[row,
                  _resident((1, d), lambda i: (0, 0)),
                  _resident((n_slots, d), lambda i: (0, 0)),
                  _resident((None, d, 4 * d), lambda i: (layer, 0, 0))],
        out_specs=[row, row, row, row],
        out_shape=[out, out, out, out],
        compiler_params=_params("parallel"),
        name="hgrn_in",
    )(x, g, a_lb, w_in)


def _hgrn_chunk(refs, e, rows, size, exact):
    q_ref, fg_ref, v_ref, og_ref, gon_ref, o_ref, st_ref = refs
    hd = HGRN_HEAD_DIM
    ri = lax.broadcasted_iota(jnp.int32, (size, size), 0)
    ci = lax.broadcasted_iota(jnp.int32, (size, size), 1)
    causal = ri >= ci
    tri = jnp.where(causal, 1.0, 0.0).astype(BF16)
    fg = fg_ref[e, rows, :]
    kk = 1.0 - fg
    hi, mid, lo = _split3(jnp.log(fg))
    gcum = _dot(tri, hi) + _dot(tri, mid) + _dot(tri, lo)
    g_last = gcum[size - 1:size, :]
    q = q_ref[e, rows, :]
    q_dec = (q * jnp.exp(gcum)).astype(BF16)
    k_dec = (kk * jnp.exp(g_last - gcum)).astype(BF16)
    s_dec = jnp.exp(g_last)
    v = v_ref[e, rows, :].astype(BF16)
    og = og_ref[e, rows, :]
    gon = gon_ref[...]
    if exact:
        row = lax.broadcasted_iota(jnp.int32, gcum.shape, 0)
        pair_terms = [q * jnp.exp(jnp.where(row >= s, gcum - gcum[s:s + 1], 0.0)) * kk[s:s + 1] for s in range(size)]
    else:
        k_inv = (kk * jnp.exp(-gcum)).astype(BF16)

    def first_products(h, _):
        hs = slice(h * hd, (h + 1) * hd)
        st = st_ref[e, h]
        if exact:
            att = jnp.zeros((size, size), F32)
            for s, t in enumerate(pair_terms):
                att = att + jnp.where(ci == s, jnp.sum(t[:, hs], axis=-1, keepdims=True), 0.0)
        else:
            att = _dot_nt(q_dec[:, hs], k_inv[:, hs])
        inter = _dot_nt(q_dec[:, hs], st.astype(BF16))
        st_ref[e, h] = st * s_dec[:, hs] + _dot_tn(v[:, hs], k_dec[:, hs])
        return att, inter

    def outputs(h, prods):
        hs = slice(h * hd, (h + 1) * hd)
        att, inter = prods
        o = _dot(jnp.where(causal, att, 0.0).astype(BF16), v[:, hs]) + inter
        on = o * lax.rsqrt(jnp.mean(o * o, axis=-1, keepdims=True) + EPS)
        o_ref[e, rows, hs] = (on * gon[:, hs] * og[:, hs]).astype(o_ref.dtype)

    return (first_products, outputs), g_last


def _hgrn_scan_kernel(*refs, chunk, n_sub, has_init):
    if has_init:
        q_ref, fg_ref, v_ref, og_ref, gon_ref, s0_ref, o_ref, sout_ref, st_ref, st0_ref = refs
    else:
        q_ref, fg_ref, v_ref, og_ref, gon_ref, o_ref, sout_ref, st_ref, st0_ref = refs
        s0_ref = None
    n_elems, n_heads = st_ref.shape[0], st_ref.shape[1]
    c = pl.program_id(1)
    chunk_refs = (q_ref, fg_ref, v_ref, og_ref, gon_ref, o_ref, st_ref)

    @pl.when(c == 0)
    def _():
        if has_init:
            for e in range(n_elems):
                for h in range(n_heads):
                    st_ref[e, h] = s0_ref[e, h].T
        else:
            st_ref[...] = jnp.zeros(st_ref.shape, F32)

    st0_ref[...] = st_ref[...]
    stages, worst = [], None
    for e in range(n_elems):
        for s in range(n_sub):
            fns, g_last = _hgrn_chunk(chunk_refs, e, slice(s * chunk, (s + 1) * chunk), chunk, exact=False)
            worst = g_last if worst is None else jnp.minimum(worst, g_last)
            stages += [[functools.partial(fn, h) for fn in fns] for h in range(n_heads)]
    _emit_skewed(len(stages), [lambda i, st, k=k: stages[i][k](st) for k in range(2)], lag=HGRN_STAGE_LAG)

    @pl.when(jnp.min(worst) < -HGRN_DECAY_LIMIT)
    def _():
        st_ref[...] = st0_ref[...]
        small = min(HGRN_EXACT_CHUNK, chunk * n_sub)

        def redo(i, carry):
            rows = pl.ds(pl.multiple_of(i * small, small), small)
            for e in range(n_elems):
                fns, _ = _hgrn_chunk(chunk_refs, e, rows, small, exact=True)
                for h in range(n_heads):
                    fns[1](h, fns[0](h, None))
            return carry

        lax.fori_loop(0, chunk * n_sub // small, redo, 0)

    @pl.when(c == pl.num_programs(1) - 1)
    def _():
        for e in range(n_elems):
            for h in range(n_heads):
                sout_ref[e, h] = st_ref[e, h].T


def _hgrn_scan(q, fg, v, og, g_onorm, s0, chunk, n_sub, n_elems, out_dtype):
    b, l, d = q.shape
    n_heads = d // HGRN_HEAD_DIM
    rows = chunk * n_sub
    blk = pl.BlockSpec((n_elems, rows, d), lambda i, c: (i, c, 0))
    st_shape = (n_elems, n_heads, HGRN_HEAD_DIM, HGRN_HEAD_DIM)
    st_blk = pl.BlockSpec(st_shape, lambda i, c: (i, 0, 0, 0))
    in_specs = [blk, blk, blk, blk, _resident((1, d), lambda i, c: (0, 0))]
    args = [q, fg, v, og, g_onorm]
    if s0 is not None:
        in_specs.append(st_blk)
        args.append(s0)
    return pl.pallas_call(
        functools.partial(_hgrn_scan_kernel, chunk=chunk, n_sub=n_sub, has_init=s0 is not None),
        grid=(b // n_elems, l // rows),
        in_specs=in_specs,
        out_specs=[blk, st_blk],
        out_shape=[jax.ShapeDtypeStruct((b, l, d), out_dtype),
                   jax.ShapeDtypeStruct((b, n_heads, HGRN_HEAD_DIM, HGRN_HEAD_DIM), F32)],
        scratch_shapes=[pltpu.VMEM(st_shape, F32), pltpu.VMEM(st_shape, F32)],
        compiler_params=_params("parallel", "arbitrary"),
        name="hgrn_scan",
    )(*args)


def _layer_tail_kernel(*refs, d_ff, ff_chunk, final):
    (h_ref, a_ref, p_ref, wo_ref, gffn_ref, win_ref, wout_ref, gple_ref, wpe_ref, wpg_ref) = refs[:10]
    h1 = h_ref[...] + _dot(a_ref[...].astype(BF16), wo_ref[...])
    hn = _rms(h1, gffn_ref[...]).astype(BF16)
    h2 = h1
    for c0 in range(0, d_ff, ff_chunk):
        c1 = min(c0 + ff_chunk, d_ff)
        gate = _dot(hn, win_ref[:, c0:c1])
        up = _dot(hn, win_ref[:, d_ff + c0:d_ff + c1])
        act = (gate * _sigmoid(gate) * up).astype(BF16)
        h2 = h2 + _dot(act, wout_ref[c0:c1, :])
    pgate = _sigmoid(_dot(_rms(h2, gple_ref[...]).astype(BF16), wpg_ref[...]))
    h3 = h2 + _dot(p_ref[...].astype(BF16), wpe_ref[...]) * pgate
    if final:
        gfin_ref, y_ref = refs[10:]
        y_ref[...] = _rms(h3, gfin_ref[...])
    else:
        gkv_ref, wkv_ref, gq_ref, wq_ref, h_out_ref, k_ref, v_ref, q_ref = refs[10:18]
        d = h3.shape[1]
        h_out_ref[...] = h3
        hkv = _rms(h3, gkv_ref[...]).astype(BF16)
        if len(refs) > 18:
            kb_ref, vb_ref = refs[18:]
            kt = _dot_nt(wkv_ref[0:d, :], hkv)
            vt = _dot_nt(wkv_ref[d:2 * d, :], hkv)
            k_ref[0] = kt
            v_ref[0] = vt
            kb_ref[0, 0] = kt.astype(BF16)
            vb_ref[0, 0] = vt.astype(BF16)
        else:
            k_ref[...] = _dot(hkv, wkv_ref[:, 0:d])
            v_ref[...] = _dot(hkv, wkv_ref[:, d:2 * d])
        hq = _rms(h3, gq_ref[...]).astype(BF16)
        q_ref[...] = (_dot(hq, wq_ref[...]) * (LOG2E * SB_HEAD_DIM ** -0.5)).astype(q_ref.dtype)


def _layer_tail(h, a, p, layer, w, tm, final, q_dtype=BF16, seq_len=None):
    t, d = h.shape
    ple = p.shape[-1]
    d_ff = w["w_ffn_out"].shape[1]
    row = pl.BlockSpec((tm, d), lambda i: (i, 0))
    vec = _resident((1, d), lambda i: (0, 0))

    def stacked(arr, idx):
        return _resident((None,) + arr.shape[1:], lambda i: (idx,) + (0,) * (arr.ndim - 1))

    j = layer - (w["w_ffn_in"].shape[0] - w["w_b_o"].shape[0])
    w_o = w["w_b_o"] if final else w["w_a_o"]
    in_specs = [row, row, pl.BlockSpec((None, tm, ple), lambda i: (layer, i, 0)),
                stacked(w_o, j if final else layer), vec, stacked(w["w_ffn_in"], layer),
                stacked(w["w_ffn_out"], layer), vec, stacked(w["w_ple_in"], layer), stacked(w["w_ple_gate"], layer)]
    args = [h, a, p, w_o, w["g_ffn"][layer][None], w["w_ffn_in"], w["w_ffn_out"], w["g_ple"][layer][None],
            w["w_ple_in"], w["w_ple_gate"]]
    if final:
        in_specs += [vec]
        args += [w["g_final"][None]]
        out_specs = [row]
        out_shape = [jax.ShapeDtypeStruct((t, d), F32)]
    else:
        w_kv = w["w_kv"] if seq_len is None else w["w_kv_t"]
        in_specs += [vec, _resident(w_kv.shape, lambda i: (0, 0)), vec, stacked(w["w_b_q"], 0)]
        args += [w["g_kv"][None], w_kv, w["g_mix"][layer + 1][None], w["w_b_q"]]
        if seq_len is None:
            out_specs = [row, row, row, row]
            out_shape = [jax.ShapeDtypeStruct((t, d), F32)] * 3 + [jax.ShapeDtypeStruct((t, d), q_dtype)]
        else:
            nb, nc = t // seq_len, seq_len // tm
            kv_t = pl.BlockSpec((1, d, tm), lambda i: (i // nc, 0, i % nc))
            kv_blocks = pl.BlockSpec((1, 1, d, tm), lambda i: (i // nc, i % nc, 0, 0))
            out_specs = [row, kv_t, kv_t, row, kv_blocks, kv_blocks]
            out_shape = ([jax.ShapeDtypeStruct((t, d), F32)] + [jax.ShapeDtypeStruct((nb, d, seq_len), F32)] * 2
                         + [jax.ShapeDtypeStruct((t, d), q_dtype)]
                         + [jax.ShapeDtypeStruct((nb, nc, d, tm), BF16)] * 2)
    return pl.pallas_call(
        functools.partial(_layer_tail_kernel, d_ff=d_ff, ff_chunk=FFN_CHUNK, final=final),
        grid=(t // tm,),
        in_specs=in_specs,
        out_specs=out_specs,
        out_shape=out_shape,
        compiler_params=_params("parallel"),
        name="layer_tail_final" if final else "layer_tail_kvq",
    )(*args)


def _strict_upper(n):
    ji = lax.broadcasted_iota(jnp.int32, (n, n), 0)
    si = lax.broadcasted_iota(jnp.int32, (n, n), 1)
    return jnp.where(ji > si, 1.0, 0.0).astype(BF16)


def _sb_logs(z, visible):
    m = jnp.minimum(z, 0.0)
    d = m - z
    l1p = jnp.log2(1.0 + jnp.exp2(m + d))
    log_1mb = d - l1p
    if visible is not None:
        log_1mb = jnp.where(visible, log_1mb, 0.0)
    return m - l1p, log_1mb


def _sb_weights(log_b, log_1mb, tail, carry, visible):
    carry_b = jnp.concatenate([carry] * (log_b.shape[1] // LANES), axis=1)
    a = jnp.exp2(log_b + tail + carry_b)
    if visible is not None:
        a = jnp.where(visible, a, 0.0)
    return a, carry + (tail[:, 0:1] + log_1mb[:, 0:1])


def _sb_block(z, upper, carry, visible):
    log_b, log_1mb = _sb_logs(z, visible)
    tail = _dot(log_1mb.astype(BF16), upper)
    return _sb_weights(log_b, log_1mb, tail, carry, visible)


def _emit_skewed(n_items, stages, lag=1):
    state = [None] * n_items
    for t in range(n_items + (len(stages) - 1) * lag):
        for s in reversed(range(len(stages))):
            i = t - s * lag
            if 0 <= i < n_items:
                state[i] = stages[s](i, state[i])


def _sb_prompt_kernel(bias_ref, q_ref, k_ref, v_ref, o_ref, qh_ref, acc_ref, car_ref, *, blk, n_heads):
    hg = pl.program_id(1)
    qi = pl.program_id(2)

    lane = lax.broadcasted_iota(jnp.int32, (blk, LANES), 1)
    first = lane < SB_HEAD_DIM
    for j in range(n_heads):
        qt = q_ref[0, :, (j // 2) * LANES:(j // 2 + 1) * LANES]
        qh_ref[j, :, 0:LANES] = jnp.where(first if j % 2 == 0 else ~first, qt, jnp.zeros_like(qt))
        bias = jnp.full((blk, LANES), bias_ref[n_heads * hg + j], F32)
        hi = bias.astype(BF16).astype(F32)
        qh_ref[j, :, LANES:2 * LANES] = jnp.where(lane == 0, hi, jnp.where(lane == 1, bias - hi, 0.0)).astype(BF16)
    ones_rows = jnp.where(lax.broadcasted_iota(jnp.int32, (LANES, blk), 0) < 2, 1.0, 0.0).astype(BF16)
    upper = _strict_upper(blk)
    ti = lax.broadcasted_iota(jnp.int32, (blk, blk), 0)
    si = lax.broadcasted_iota(jnp.int32, (blk, blk), 1)
    strictly_before = si < ti
    acc_ref[...] = jnp.zeros(acc_ref.shape, F32)
    car_ref[...] = jnp.zeros(car_ref.shape, F32)

    def process(kb, visible):

        def tile(j):
            return slice((j // 2) * LANES, (j // 2 + 1) * LANES)

        def logs(j, _):
            z = _dot(qh_ref[j], jnp.concatenate([k_ref[0, kb, tile(j), :], ones_rows], axis=0))
            return _sb_logs(z, visible)

        def tails(j, st):
            return st + (_dot(st[1].astype(BF16), upper),)

        def weights(j, st):
            a, car_ref[j] = _sb_weights(*st, car_ref[j], visible)
            acc_ref[j] += _dot_nt(a.astype(BF16), v_ref[0, kb, tile(j), :])

        _emit_skewed(n_heads, [logs, tails, weights], lag=PROMPT_STAGE_LAG)

    process(qi, strictly_before)

    def body(i, carry):
        process(qi - 1 - i, None)
        return carry

    lax.fori_loop(0, qi, body, 0)
    for t in range(n_heads // 2):
        o_ref[0, :, t * LANES:(t + 1) * LANES] = jnp.where(first, acc_ref[2 * t], acc_ref[2 * t + 1]).astype(o_ref.dtype)


def _sb_prompt(q, k_t, v_t, bias, blk, n_heads):
    b, l, d = q.shape
    width = n_heads * SB_HEAD_DIM
    q_blk = pl.BlockSpec((1, blk, width), lambda i, hg, qi, bias: (i, qi, hg))
    kv_blk = pl.BlockSpec((1, l // blk, width, blk), lambda i, hg, qi, bias: (i, 0, hg, 0))
    return pl.pallas_call(
        functools.partial(_sb_prompt_kernel, blk=blk, n_heads=n_heads),
        grid_spec=pltpu.PrefetchScalarGridSpec(
            num_scalar_prefetch=1,
            grid=(b, d // width, l // blk),
            in_specs=[q_blk, kv_blk, kv_blk],
            out_specs=q_blk,
            scratch_shapes=[pltpu.VMEM((n_heads, blk, 2 * LANES), BF16),
                            pltpu.VMEM((n_heads, blk, LANES), F32), pltpu.VMEM((n_heads, blk, LANES), F32)]),
        out_shape=jax.ShapeDtypeStruct((b, l, d), BF16),
        compiler_params=_params("parallel", "parallel", "arbitrary"),
        name="sb_prompt",
    )(bias, q, k_t, v_t)


def _sb_sample_kernel(pt_ref, q_ref, kn_ref, vn_ref, bias_ref, *refs, n_step_pages, page, n_heads, single_step):
    k_refs = refs[:n_step_pages]
    v_refs = refs[n_step_pages:2 * n_step_pages]
    o_ref, qbd_ref, kn_buf, vn_buf, acc_ref, car_ref = refs[2 * n_step_pages:]
    del pt_ref
    step = pl.program_id(1)
    n_q, d = q_ref.shape[1], q_ref.shape[2]
    rows = n_q * n_heads
    upper = _strict_upper(page)
    bias = bias_ref[...]
    hrow = lax.broadcasted_iota(jnp.int32, (n_heads, d), 0)
    hlane = lax.broadcasted_iota(jnp.int32, (n_heads, d), 1) // SB_HEAD_DIM
    own_head = hrow == hlane

    def run(blocks):
        def logs(i, _):
            return _sb_logs(blocks[i][0](qbd_ref[...]) + bias, blocks[i][2])

        def tails(i, st):
            return st + (_dot(st[1].astype(BF16), upper),)

        def weights(i, st):
            a, car_ref[...] = _sb_weights(*st, car_ref[...], blocks[i][2])
            acc_ref[...] += blocks[i][1](a.astype(BF16))

        _emit_skewed(len(blocks), [logs, tails, weights])

    def start():
        q = q_ref[0]
        for qi in range(n_q):
            qrow = jnp.broadcast_to(q[qi:qi + 1, :], (n_heads, d))
            qbd_ref[qi * n_heads:(qi + 1) * n_heads, :] = jnp.where(own_head, qrow, 0.0).astype(BF16)
        acc_ref[...] = jnp.zeros(acc_ref.shape, F32)
        car_ref[...] = jnp.zeros(car_ref.shape, F32)
        kn_buf[...] = jnp.zeros(kn_buf.shape, F32)
        vn_buf[...] = jnp.zeros(vn_buf.shape, F32)
        kn_buf[0:n_q, :] = kn_ref[0]
        vn_buf[0:n_q, :] = vn_ref[0]

    r_q = lax.broadcasted_iota(jnp.int32, (rows, page), 0) // n_heads
    s_k = lax.broadcasted_iota(jnp.int32, (rows, page), 1)
    new_block = (lambda qbd: _dot_nt(qbd, kn_buf[...].astype(BF16)),
                 lambda a: _dot(a, vn_buf[...].astype(BF16)), s_k < r_q)
    pages = [(lambda qbd, r=kr: _dot(qbd, r[0].astype(BF16)), lambda a, r=vr: _dot_nt(a, r[0].astype(BF16)), None)
             for kr, vr in zip(k_refs, v_refs)]
    if single_step:
        start()
        run([new_block] + pages)
    else:
        @pl.when(step == 0)
        def _():
            start()
            run([new_block])

        run(pages)

    @pl.when(step == pl.num_programs(1) - 1)
    def _():
        out_rows = []
        for qi in range(n_q):
            blk = acc_ref[qi * n_heads:(qi + 1) * n_heads, :]
            out_rows.append(jnp.sum(jnp.where(own_head, blk, 0.0), axis=0, keepdims=True))
        o_ref[0] = jnp.concatenate(out_rows, axis=0)


def _sb_sample(q, k_new, v_new, bias_rows, cache_k_t, cache_v_t, page_table, n_step_pages):
    b, n_q, d = q.shape
    n_pages = page_table.shape[1]
    page = cache_k_t.shape[2]
    n_heads = d // SB_HEAD_DIM
    rows = n_q * n_heads
    n_steps = n_pages // n_step_pages
    tok = pl.BlockSpec((1, n_q, d), lambda i, s, pt: (i, 0, 0))

    def page_spec(j):
        return pl.BlockSpec((1, d, page),
                            lambda i, s, pt: (pt[i * n_pages + (n_pages - 1 - (s * n_step_pages + j))], 0, 0))

    pages = [page_spec(j) for j in range(n_step_pages)]
    return pl.pallas_call(
        functools.partial(_sb_sample_kernel, n_step_pages=n_step_pages, page=page, n_heads=n_heads,
                          single_step=n_steps == 1),
        grid_spec=pltpu.PrefetchScalarGridSpec(
            num_scalar_prefetch=1,
            grid=(b, n_steps),
            in_specs=[tok, tok, tok, _resident(bias_rows.shape, lambda i, s, pt: (0, 0))] + pages + pages,
            out_specs=tok,
            scratch_shapes=[pltpu.VMEM((rows, d), BF16), pltpu.VMEM((page, d), F32), pltpu.VMEM((page, d), F32),
                            pltpu.VMEM((rows, d), F32), pltpu.VMEM((rows, LANES), F32)]),
        out_shape=jax.ShapeDtypeStruct((b, n_q, d), F32),
        compiler_params=_params("parallel", "arbitrary"),
        name="sb_sample",
    )(page_table.reshape(-1), q, k_new, v_new, bias_rows,
      *([cache_k_t] * n_step_pages), *([cache_v_t] * n_step_pages))


HGRN_CHUNK = 64
HGRN_SUBCHUNKS = 4
HGRN_STAGE_LAG = 3
HGRN_DECAY_LIMIT = 80.0
HGRN_EXACT_CHUNK = 16
SAMPLE_PAD = 8
SAMPLE_ELEMS = 4
FINAL_TILE = 512
FFN_CHUNK = 1024
PROMPT_BLOCK = 256
PROMPT_HEADS = 16
PROMPT_STAGE_LAG = 2
STEP_PAGES = 16


def _row_tile(t, want):
    return want if t % want == 0 else t


def _trunk(x, p, state0, past, w):
    b, l, d = x.shape
    t = b * l
    x2 = x.reshape(t, d)
    p2 = p.reshape(p.shape[0], t, p.shape[-1])
    q, fg, v, og = _hgrn_in(x2, w["g_mix"][0][None], w["a_lb"], w["w_a_in"], 0, _row_tile(t, 512))
    q, fg, v, og = (a.reshape(b, l, d) for a in (q, fg, v, og))
    if l % (HGRN_CHUNK * HGRN_SUBCHUNKS) == 0:
        o, s_fin = _hgrn_scan(q, fg, v, og, w["g_a_onorm"][0][None], state0, HGRN_CHUNK, HGRN_SUBCHUNKS, 1, BF16)
    else:
        pad = ((0, 0), (0, SAMPLE_PAD - l), (0, 0))
        n_elems = SAMPLE_ELEMS if b % SAMPLE_ELEMS == 0 else 1
        o, s_fin = _hgrn_scan(jnp.pad(q, pad), jnp.pad(fg, pad, constant_values=1.0), jnp.pad(v, pad),
                              jnp.pad(og, pad), w["g_a_onorm"][0][None], state0, SAMPLE_PAD, 1, n_elems, F32)
        o = o[:, :l]
    n_heads = d // SB_HEAD_DIM
    if past is None:
        tm = PROMPT_BLOCK
        h, k_t, v_t, q1, kb_t, vb_t = _layer_tail(x2, o.reshape(t, d), p2, 0, w, tm, final=False, seq_len=l)
        attn = _sb_prompt(q1.reshape(b, l, d), kb_t, vb_t, w["sb_bias"][0], PROMPT_BLOCK, PROMPT_HEADS)
        k, v_kv = (jnp.transpose(a.reshape(b, n_heads, SB_HEAD_DIM, l), (0, 3, 1, 2)) for a in (k_t, v_t))
    else:
        tm = _row_tile(t, 256)
        h, k, v_kv, q1 = _layer_tail(x2, o.reshape(t, d), p2, 0, w, tm, final=False, q_dtype=F32)
        cache_k, cache_v, page_table = past
        bias_rows = jnp.broadcast_to(jnp.tile(w["sb_bias"][0], l)[:, None], (l * n_heads, LANES))
        cache_k_t, cache_v_t = (jnp.transpose(c, (0, 2, 3, 1)).reshape(c.shape[0], d, c.shape[1])
                                for c in (cache_k, cache_v))
        attn = _sb_sample(q1.reshape(b, l, d), k.reshape(b, l, d), v_kv.reshape(b, l, d), bias_rows,
                          cache_k_t, cache_v_t, page_table, min(STEP_PAGES, page_table.shape[1]))
        k, v_kv = (a.reshape(b, l, n_heads, SB_HEAD_DIM) for a in (k, v_kv))
    (y,) = _layer_tail(h, attn.reshape(t, d), p2, 1, w, _row_tile(t, FINAL_TILE), final=True)
    return y.reshape(b, l, d), s_fin[None], k, v_kv


def kernel(x_prompt, x_sample, p_prompt, p_sample, state_hgrn, cache_k, cache_v, page_table, a_lb, w_a_in,
           g_a_onorm, w_a_o, g_kv, w_kv, w_b_q, w_b_o, sb_bias, g_mix, g_ffn, w_ffn_in, w_ffn_out, g_ple,
           w_ple_in, w_ple_gate, g_final):
    w = dict(a_lb=a_lb, g_a_onorm=g_a_onorm, g_kv=g_kv, sb_bias=sb_bias * LOG2E, g_mix=g_mix, g_ffn=g_ffn, g_ple=g_ple,
             g_final=g_final,
             w_a_in=w_a_in.astype(BF16), w_a_o=w_a_o.astype(BF16), w_kv=w_kv.astype(BF16),
             w_kv_t=w_kv.T.astype(BF16),
             w_b_q=w_b_q.astype(BF16), w_b_o=w_b_o.astype(BF16), w_ffn_in=w_ffn_in.astype(BF16),
             w_ffn_out=w_ffn_out.astype(BF16), w_ple_in=w_ple_in.astype(BF16),
             w_ple_gate=w_ple_gate.astype(BF16))
    y_p, st_p, k_p, v_p = _trunk(x_prompt, p_prompt, None, None, w)
    y_s, st_s, k_s, v_s = _trunk(x_sample, p_sample, state_hgrn[0], (cache_k, cache_v, page_table), w)
    return (y_p, y_s, st_p, st_s, k_p, v_p, k_s, v_s)
```

```python
import functools

import jax
import jax.numpy as jnp
from jax import lax
from jax.experimental import pallas as pl
from jax.experimental.pallas import tpu as pltpu

F32 = jnp.float32
BF16 = jnp.bfloat16
EPS = 1e-6
LOG2E = 1.4426950408889634
HGRN_HEAD_DIM = 128
SB_HEAD_DIM = 64
LANES = 128
VMEM_LIMIT_BYTES = 56 * 1024 * 1024


def _sigmoid(x):
    return 1.0 / (1.0 + jnp.exp(-x))


def _rms(x, g):
    return x * lax.rsqrt(jnp.mean(x * x, axis=-1, keepdims=True) + EPS) * g


def _dot(a, b):
    return jnp.dot(a, b, preferred_element_type=F32)


def _dot_nt(a, b):
    return lax.dot_general(a, b, (((1,), (1,)), ((), ())), preferred_element_type=F32)


def _dot_tn(a, b):
    return lax.dot_general(a, b, (((0,), (0,)), ((), ())), preferred_element_type=F32)


def _split3(x):
    hi = x.astype(BF16)
    r = x - hi.astype(F32)
    mid = r.astype(BF16)
    lo = (r - mid.astype(F32)).astype(BF16)
    return hi, mid, lo


def _params(*sem):
    return pltpu.CompilerParams(dimension_semantics=sem, vmem_limit_bytes=VMEM_LIMIT_BYTES)


def _resident(shape, index_map):
    return pl.BlockSpec(shape, index_map, pipeline_mode=pl.Buffered(1))


def _hgrn_in_kernel(x_ref, g_ref, alb_ref, w_ref, q_ref, fg_ref, v_ref, og_ref, *, layer):
    tm, d = x_ref.shape
    a = alb_ref[...]
    e = jnp.exp(a - jnp.max(a, axis=0, keepdims=True))
    lb = jnp.sum(e[0:layer + 1], axis=0, keepdims=True) / jnp.sum(e, axis=0, keepdims=True)
    n_groups = 2 if tm % 16 == 0 else 1
    groups = [slice(g * (tm // n_groups), (g + 1) * (tm // n_groups)) for g in range(n_groups)]
    hn = [_rms(x_ref[rs, :], g_ref[...]).astype(BF16) for rs in groups]
    for g, rs in enumerate(groups):
        pq = _dot(hn[g], w_ref[:, 0:d])
        q_ref[rs, :] = pq * _sigmoid(pq)
        pf = _dot(hn[g], w_ref[:, d:2 * d])
        fg_ref[rs, :] = lb + (1.0 - lb) * _sigmoid(pf)
        v_ref[rs, :] = _dot(hn[g], w_ref[:, 2 * d:3 * d])
        po = _dot(hn[g], w_ref[:, 3 * d:4 * d])
        og_ref[rs, :] = po * _sigmoid(po)


def _hgrn_in(x, g, a_lb, w_in, layer, tm):
    t, d = x.shape
    n_slots = a_lb.shape[0]
    row = pl.BlockSpec((tm, d), lambda i: (i, 0))
    out = jax.ShapeDtypeStruct((t, d), F32)
    return pl.pallas_call(
        functools.partial(_hgrn_in_kernel, layer=layer),
        grid=(t // tm,),
        in_specs=[row,
                  _resident((1, d), lambda i: (0, 0)),
                  _resident((n_slots, d), lambda i: (0, 0)),
                  _resident((None, d, 4 * d), lambda i: (layer, 0, 0))],
        out_specs=[row, row, row, row],
        out_shape=[out, out, out, out],
        compiler_params=_params("parallel"),
        name="hgrn_in",
    )(x, g, a_lb, w_in)


def _hgrn_chunk(refs, e, rows, size, exact):
    q_ref, fg_ref, v_ref, og_ref, gon_ref, o_ref, st_ref = refs
    hd = HGRN_HEAD_DIM
    ri = lax.broadcasted_iota(jnp.int32, (size, size), 0)
    ci = lax.broadcasted_iota(jnp.int32, (size, size), 1)
    causal = ri >= ci
    tri = jnp.where(causal, 1.0, 0.0).astype(BF16)
    fg = fg_ref[e, rows, :]
    kk = 1.0 - fg
    hi, mid, lo = _split3(jnp.log(fg))
    gcum = _dot(tri, hi) + _dot(tri, mid) + _dot(tri, lo)
    g_last = gcum[size - 1:size, :]
    q = q_ref[e, rows, :]
    q_dec = (q * jnp.exp(gcum)).astype(BF16)
    k_dec = (kk * jnp.exp(g_last - gcum)).astype(BF16)
    s_dec = jnp.exp(g_last)
    v = v_ref[e, rows, :].astype(BF16)
    og = og_ref[e, rows, :]
    gon = gon_ref[...]
    if exact:
        row = lax.broadcasted_iota(jnp.int32, gcum.shape, 0)
        pair_terms = [q * jnp.exp(jnp.where(row >= s, gcum - gcum[s:s + 1], 0.0)) * kk[s:s + 1] for s in range(size)]
    else:
        k_inv = (kk * jnp.exp(-gcum)).astype(BF16)

    def first_products(h, _):
        hs = slice(h * hd, (h + 1) * hd)
        st = st_ref[e, h]
        if exact:
            att = jnp.zeros((size, size), F32)
            for s, t in enumerate(pair_terms):
                att = att + jnp.where(ci == s, jnp.sum(t[:, hs], axis=-1, keepdims=True), 0.0)
        else:
            att = _dot_nt(q_dec[:, hs], k_inv[:, hs])
        inter = _dot_nt(q_dec[:, hs], st.astype(BF16))
        st_ref[e, h] = st * s_dec[:, hs] + _dot_tn(v[:, hs], k_dec[:, hs])
        return att, inter

    def outputs(h, prods):
        hs = slice(h * hd, (h + 1) * hd)
        att, inter = prods
        o = _dot(jnp.where(causal, att, 0.0).astype(BF16), v[:, hs]) + inter
        on = o * lax.rsqrt(jnp.mean(o * o, axis=-1, keepdims=True) + EPS)
        o_ref[e, rows, hs] = (on * gon[:, hs] * og[:, hs]).astype(o_ref.dtype)

    return (first_products, outputs), g_last


def _hgrn_scan_kernel(*refs, chunk, n_sub, has_init):
    if has_init:
        q_ref, fg_ref, v_ref, og_ref, gon_ref, s0_ref, o_ref, sout_ref, st_ref, st0_ref = refs
    else:
        q_ref, fg_ref, v_ref, og_ref, gon_ref, o_ref, sout_ref, st_ref, st0_ref = refs
        s0_ref = None
    n_elems, n_heads = st_ref.shape[0], st_ref.shape[1]
    c = pl.program_id(1)
    chunk_refs = (q_ref, fg_ref, v_ref, og_ref, gon_ref, o_ref, st_ref)

    @pl.when(c == 0)
    def _():
        if has_init:
            for e in range(n_elems):
                for h in range(n_heads):
                    st_ref[e, h] = s0_ref[e, h].T
        else:
            st_ref[...] = jnp.zeros(st_ref.shape, F32)

    st0_ref[...] = st_ref[...]
    stages, worst = [], None
    for e in range(n_elems):
        for s in range(n_sub):
            fns, g_last = _hgrn_chunk(chunk_refs, e, slice(s * chunk, (s + 1) * chunk), chunk, exact=False)
            worst = g_last if worst is None else jnp.minimum(worst, g_last)
            stages += [[functools.partial(fn, h) for fn in fns] for h in range(n_heads)]
    _emit_skewed(len(stages), [lambda i, st, k=k: stages[i][k](st) for k in range(2)], lag=HGRN_STAGE_LAG)

    @pl.when(jnp.min(worst) < -HGRN_DECAY_LIMIT)
    def _():
        st_ref[...] = st0_ref[...]
        small = min(HGRN_EXACT_CHUNK, chunk * n_sub)

        def redo(i, carry):
            rows = pl.ds(pl.multiple_of(i * small, small), small)
            for e in range(n_elems):
                fns, _ = _hgrn_chunk(chunk_refs, e, rows, small, exact=True)
                for h in range(n_heads):
                    fns[1](h, fns[0](h, None))
            return carry

        lax.fori_loop(0, chunk * n_sub // small, redo, 0)

    @pl.when(c == pl.num_programs(1) - 1)
    def _():
        for e in range(n_elems):
            for h in range(n_heads):
                sout_ref[e, h] = st_ref[e, h].T


def _hgrn_scan(q, fg, v, og, g_onorm, s0, chunk, n_sub, n_elems, out_dtype):
    b, l, d = q.shape
    n_heads = d // HGRN_HEAD_DIM
    rows = chunk * n_sub
    blk = pl.BlockSpec((n_elems, rows, d), lambda i, c: (i, c, 0))
    st_shape = (n_elems, n_heads, HGRN_HEAD_DIM, HGRN_HEAD_DIM)
    st_blk = pl.BlockSpec(st_shape, lambda i, c: (i, 0, 0, 0))
    in_specs = [blk, blk, blk, blk, _resident((1, d), lambda i, c: (0, 0))]
    args = [q, fg, v, og, g_onorm]
    if s0 is not None:
        in_specs.append(st_blk)
        args.append(s0)
    return pl.pallas_call(
        functools.partial(_hgrn_scan_kernel, chunk=chunk, n_sub=n_sub, has_init=s0 is not None),
        grid=(b // n_elems, l // rows),
        in_specs=in_specs,
        out_specs=[blk, st_blk],
        out_shape=[jax.ShapeDtypeStruct((b, l, d), out_dtype),
                   jax.ShapeDtypeStruct((b, n_heads, HGRN_HEAD_DIM, HGRN_HEAD_DIM), F32)],
        scratch_shapes=[pltpu.VMEM(st_shape, F32), pltpu.VMEM(st_shape, F32)],
        compiler_params=_params("parallel", "arbitrary"),
        name="hgrn_scan",
    )(*args)


def _layer_tail_kernel(*refs, d_ff, ff_chunk, final, n_groups):
    (h_ref, a_ref, p_ref, wo_ref, gffn_ref, win_ref, wout_ref, gple_ref, wpe_ref, wpg_ref) = refs[:10]
    tm, d = h_ref.shape
    size = tm // n_groups
    groups = [slice(g * size, (g + 1) * size) for g in range(n_groups)]
    st = [dict() for _ in groups]

    def mix(g):
        h1 = h_ref[groups[g], :] + _dot(a_ref[groups[g], :].astype(BF16), wo_ref[...])
        st[g].update(h=h1, hn=_rms(h1, gffn_ref[...]).astype(BF16))

    def ffn_up(g, c0, c1):
        gate = _dot(st[g]["hn"], win_ref[:, c0:c1])
        up = _dot(st[g]["hn"], win_ref[:, d_ff + c0:d_ff + c1])
        st[g]["act"] = (gate * _sigmoid(gate) * up).astype(BF16)

    def ffn_down(g, c0, c1):
        st[g]["h"] = st[g]["h"] + _dot(st[g]["act"], wout_ref[c0:c1, :])

    def ple_products(g):
        st[g]["pg"] = _dot(_rms(st[g]["h"], gple_ref[...]).astype(BF16), wpg_ref[...])
        st[g]["pe"] = _dot(p_ref[groups[g], :].astype(BF16), wpe_ref[...])

    def ple_add(g):
        h3 = st[g]["h"] + st[g]["pe"] * _sigmoid(st[g]["pg"])
        if final:
            y_ref[groups[g], :] = _rms(h3, gfin_ref[...])
        else:
            h_out_ref[groups[g], :] = h3
            st[g].update(hkv=_rms(h3, gkv_ref[...]).astype(BF16), hq=_rms(h3, gq_ref[...]).astype(BF16))

    def kv(g):
        if len(refs) > 18:
            if g == n_groups - 1:
                hkv = jnp.concatenate([s["hkv"] for s in st], axis=0)
                kt = _dot_nt(wkv_ref[0:d, :], hkv)
                vt = _dot_nt(wkv_ref[d:2 * d, :], hkv)
                k_ref[0] = kt
                v_ref[0] = vt
                kb_ref[0, 0] = kt.astype(BF16)
                vb_ref[0, 0] = vt.astype(BF16)
        else:
            k_ref[groups[g], :] = _dot(st[g]["hkv"], wkv_ref[:, 0:d])
            v_ref[groups[g], :] = _dot(st[g]["hkv"], wkv_ref[:, d:2 * d])

    def query(g):
        q_ref[groups[g], :] = (_dot(st[g]["hq"], wq_ref[...]) * (LOG2E * SB_HEAD_DIM ** -0.5)).astype(q_ref.dtype)

    stages = [mix]
    for c0 in range(0, d_ff, ff_chunk):
        c1 = min(c0 + ff_chunk, d_ff)
        stages += [functools.partial(ffn_up, c0=c0, c1=c1), functools.partial(ffn_down, c0=c0, c1=c1)]
    stages += [ple_products, ple_add]
    if final:
        gfin_ref, y_ref = refs[10:]
    else:
        gkv_ref, wkv_ref, gq_ref, wq_ref, h_out_ref, k_ref, v_ref, q_ref = refs[10:18]
        if len(refs) > 18:
            kb_ref, vb_ref = refs[18:]
        stages += [kv, query]
    for stage in stages:
        for g in range(n_groups):
            stage(g)


def _layer_tail(h, a, p, layer, w, tm, final, q_dtype=BF16, seq_len=None):
    t, d = h.shape
    ple = p.shape[-1]
    d_ff = w["w_ffn_out"].shape[1]
    row = pl.BlockSpec((tm, d), lambda i: (i, 0))
    vec = _resident((1, d), lambda i: (0, 0))

    def stacked(arr, idx):
        return _resident((None,) + arr.shape[1:], lambda i: (idx,) + (0,) * (arr.ndim - 1))

    j = layer - (w["w_ffn_in"].shape[0] - w["w_b_o"].shape[0])
    w_o = w["w_b_o"] if final else w["w_a_o"]
    in_specs = [row, row, pl.BlockSpec((None, tm, ple), lambda i: (layer, i, 0)),
                stacked(w_o, j if final else layer), vec, stacked(w["w_ffn_in"], layer),
                stacked(w["w_ffn_out"], layer), vec, stacked(w["w_ple_in"], layer), stacked(w["w_ple_gate"], layer)]
    args = [h, a, p, w_o, w["g_ffn"][layer][None], w["w_ffn_in"], w["w_ffn_out"], w["g_ple"][layer][None],
            w["w_ple_in"], w["w_ple_gate"]]
    if final:
        in_specs += [vec]
        args += [w["g_final"][None]]
        out_specs = [row]
        out_shape = [jax.ShapeDtypeStruct((t, d), F32)]
    else:
        w_kv = w["w_kv"] if seq_len is None else w["w_kv_t"]
        in_specs += [vec, _resident(w_kv.shape, lambda i: (0, 0)), vec, stacked(w["w_b_q"], 0)]
        args += [w["g_kv"][None], w_kv, w["g_mix"][layer + 1][None], w["w_b_q"]]
        if seq_len is None:
            out_specs = [row, row, row, row]
            out_shape = [jax.ShapeDtypeStruct((t, d), F32)] * 3 + [jax.ShapeDtypeStruct((t, d), q_dtype)]
        else:
            nb, nc = t // seq_len, seq_len // tm
            kv_t = pl.BlockSpec((1, d, tm), lambda i: (i // nc, 0, i % nc))
            kv_blocks = pl.BlockSpec((1, 1, d, tm), lambda i: (i // nc, i % nc, 0, 0))
            out_specs = [row, kv_t, kv_t, row, kv_blocks, kv_blocks]
            out_shape = ([jax.ShapeDtypeStruct((t, d), F32)] + [jax.ShapeDtypeStruct((nb, d, seq_len), F32)] * 2
                         + [jax.ShapeDtypeStruct((t, d), q_dtype)]
                         + [jax.ShapeDtypeStruct((nb, nc, d, tm), BF16)] * 2)
    return pl.pallas_call(
        functools.partial(_layer_tail_kernel, d_ff=d_ff, ff_chunk=FFN_CHUNK, final=final,
                          n_groups=TAIL_GROUPS if tm % (TAIL_GROUPS * LANES) == 0 else 1),
        grid=(t // tm,),
        in_specs=in_specs,
        out_specs=out_specs,
        out_shape=out_shape,
        compiler_params=_params("parallel"),
        name="layer_tail_final" if final else "layer_tail_kvq",
    )(*args)


def _strict_upper(n):
    ji = lax.broadcasted_iota(jnp.int32, (n, n), 0)
    si = lax.broadcasted_iota(jnp.int32, (n, n), 1)
    return jnp.where(ji > si, 1.0, 0.0).astype(BF16)


def _sb_logs(z, visible):
    m = jnp.minimum(z, 0.0)
    d = m - z
    l1p = jnp.log2(1.0 + jnp.exp2(m + d))
    log_1mb = d - l1p
    if visible is not None:
        log_1mb = jnp.where(visible, log_1mb, 0.0)
    return m - l1p, log_1mb


def _sb_weights(log_b, log_1mb, tail, carry, visible):
    carry_b = jnp.concatenate([carry] * (log_b.shape[1] // LANES), axis=1)
    a = jnp.exp2(log_b + tail + carry_b)
    if visible is not None:
        a = jnp.where(visible, a, 0.0)
    return a, carry + (tail[:, 0:1] + log_1mb[:, 0:1])


def _sb_block(z, upper, carry, visible):
    log_b, log_1mb = _sb_logs(z, visible)
    tail = _dot(log_1mb.astype(BF16), upper)
    return _sb_weights(log_b, log_1mb, tail, carry, visible)


def _emit_skewed(n_items, stages, lag=1):
    state = [None] * n_items
    for t in range(n_items + (len(stages) - 1) * lag):
        for s in reversed(range(len(stages))):
            i = t - s * lag
            if 0 <= i < n_items:
                state[i] = stages[s](i, state[i])


def _sb_prompt_kernel(bias_ref, q_ref, k_ref, v_ref, o_ref, qh_ref, acc_ref, car_ref, *, blk, n_heads):
    hg = pl.program_id(1)
    qi = pl.program_id(2)

    lane = lax.broadcasted_iota(jnp.int32, (blk, LANES), 1)
    first = lane < SB_HEAD_DIM
    for j in range(n_heads):
        qt = q_ref[0, :, (j // 2) * LANES:(j // 2 + 1) * LANES]
        qh_ref[j, :, 0:LANES] = jnp.where(first if j % 2 == 0 else ~first, qt, jnp.zeros_like(qt))
        bias = jnp.full((blk, LANES), bias_ref[n_heads * hg + j], F32)
        hi = bias.astype(BF16).astype(F32)
        qh_ref[j, :, LANES:2 * LANES] = jnp.where(lane == 0, hi, jnp.where(lane == 1, bias - hi, 0.0)).astype(BF16)
    ones_rows = jnp.where(lax.broadcasted_iota(jnp.int32, (LANES, blk), 0) < 2, 1.0, 0.0).astype(BF16)
    upper = _strict_upper(blk)
    ti = lax.broadcasted_iota(jnp.int32, (blk, blk), 0)
    si = lax.broadcasted_iota(jnp.int32, (blk, blk), 1)
    strictly_before = si < ti
    acc_ref[...] = jnp.zeros(acc_ref.shape, F32)
    car_ref[...] = jnp.zeros(car_ref.shape, F32)

    def process(kb, visible):

        def tile(j):
            return slice((j // 2) * LANES, (j // 2 + 1) * LANES)

        def logs(j, _):
            z = _dot(qh_ref[j], jnp.concatenate([k_ref[0, kb, tile(j), :], ones_rows], axis=0))
            return _sb_logs(z, visible)

        def tails(j, st):
            return st + (_dot(st[1].astype(BF16), upper),)

        def weights(j, st):
            a, car_ref[j] = _sb_weights(*st, car_ref[j], visible)
            acc_ref[j] += _dot_nt(a.astype(BF16), v_ref[0, kb, tile(j), :])

        _emit_skewed(n_heads, [logs, tails, weights], lag=PROMPT_STAGE_LAG)

    process(qi, strictly_before)

    def body(i, carry):
        process(qi - 1 - i, None)
        return carry

    lax.fori_loop(0, qi, body, 0)
    for t in range(n_heads // 2):
        o_ref[0, :, t * LANES:(t + 1) * LANES] = jnp.where(first, acc_ref[2 * t], acc_ref[2 * t + 1]).astype(o_ref.dtype)


def _sb_prompt(q, k_t, v_t, bias, blk, n_heads):
    b, l, d = q.shape
    width = n_heads * SB_HEAD_DIM
    q_blk = pl.BlockSpec((1, blk, width), lambda i, hg, qi, bias: (i, qi, hg))
    kv_blk = pl.BlockSpec((1, l // blk, width, blk), lambda i, hg, qi, bias: (i, 0, hg, 0))
    return pl.pallas_call(
        functools.partial(_sb_prompt_kernel, blk=blk, n_heads=n_heads),
        grid_spec=pltpu.PrefetchScalarGridSpec(
            num_scalar_prefetch=1,
            grid=(b, d // width, l // blk),
            in_specs=[q_blk, kv_blk, kv_blk],
            out_specs=q_blk,
            scratch_shapes=[pltpu.VMEM((n_heads, blk, 2 * LANES), BF16),
                            pltpu.VMEM((n_heads, blk, LANES), F32), pltpu.VMEM((n_heads, blk, LANES), F32)]),
        out_shape=jax.ShapeDtypeStruct((b, l, d), BF16),
        compiler_params=_params("parallel", "parallel", "arbitrary"),
        name="sb_prompt",
    )(bias, q, k_t, v_t)


def _sb_sample_kernel(pt_ref, q_ref, kn_ref, vn_ref, bias_ref, *refs, n_step_pages, page, n_heads, single_step):
    k_refs = refs[:n_step_pages]
    v_refs = refs[n_step_pages:2 * n_step_pages]
    o_ref, qbd_ref, kn_buf, vn_buf, acc_ref, car_ref = refs[2 * n_step_pages:]
    del pt_ref
    step = pl.program_id(1)
    n_q, d = q_ref.shape[1], q_ref.shape[2]
    rows = n_q * n_heads
    upper = _strict_upper(page)
    bias = bias_ref[...]
    hrow = lax.broadcasted_iota(jnp.int32, (n_heads, d), 0)
    hlane = lax.broadcasted_iota(jnp.int32, (n_heads, d), 1) // SB_HEAD_DIM
    own_head = hrow == hlane

    def run(blocks):
        def logs(i, _):
            return _sb_logs(blocks[i][0](qbd_ref[...]) + bias, blocks[i][2])

        def tails(i, st):
            return st + (_dot(st[1].astype(BF16), upper),)

        def weights(i, st):
            a, car_ref[...] = _sb_weights(*st, car_ref[...], blocks[i][2])
            acc_ref[...] += blocks[i][1](a.astype(BF16))

        _emit_skewed(len(blocks), [logs, tails, weights], lag=SAMPLE_STAGE_LAG)

    def start():
        q = q_ref[0]
        for qi in range(n_q):
            qrow = jnp.broadcast_to(q[qi:qi + 1, :], (n_heads, d))
            qbd_ref[qi * n_heads:(qi + 1) * n_heads, :] = jnp.where(own_head, qrow, 0.0).astype(BF16)
        acc_ref[...] = jnp.zeros(acc_ref.shape, F32)
        car_ref[...] = jnp.zeros(car_ref.shape, F32)
        kn_buf[...] = jnp.zeros(kn_buf.shape, F32)
        vn_buf[...] = jnp.zeros(vn_buf.shape, F32)
        kn_buf[0:n_q, :] = kn_ref[0]
        vn_buf[0:n_q, :] = vn_ref[0]

    r_q = lax.broadcasted_iota(jnp.int32, (rows, page), 0) // n_heads
    s_k = lax.broadcasted_iota(jnp.int32, (rows, page), 1)
    new_block = (lambda qbd: _dot_nt(qbd, kn_buf[...].astype(BF16)),
                 lambda a: _dot(a, vn_buf[...].astype(BF16)), s_k < r_q)
    pages = [(lambda qbd, r=kr: _dot(qbd, r[0].astype(BF16)), lambda a, r=vr: _dot_nt(a, r[0].astype(BF16)), None)
             for kr, vr in zip(k_refs, v_refs)]
    if single_step:
        start()
        run([new_block] + pages)
    else:
        @pl.when(step == 0)
        def _():
            start()
            run([new_block])

        run(pages)

    @pl.when(step == pl.num_programs(1) - 1)
    def _():
        out_rows = []
        for qi in range(n_q):
            blk = acc_ref[qi * n_heads:(qi + 1) * n_heads, :]
            out_rows.append(jnp.sum(jnp.where(own_head, blk, 0.0), axis=0, keepdims=True))
        o_ref[0] = jnp.concatenate(out_rows, axis=0)


def _sb_sample(q, k_new, v_new, bias_rows, cache_k_t, cache_v_t, page_table, n_step_pages):
    b, n_q, d = q.shape
    n_pages = page_table.shape[1]
    page = cache_k_t.shape[2]
    n_heads = d // SB_HEAD_DIM
    rows = n_q * n_heads
    n_steps = n_pages // n_step_pages
    tok = pl.BlockSpec((1, n_q, d), lambda i, s, pt: (i, 0, 0))

    def page_spec(j):
        return pl.BlockSpec((1, d, page),
                            lambda i, s, pt: (pt[i * n_pages + (n_pages - 1 - (s * n_step_pages + j))], 0, 0))

    pages = [page_spec(j) for j in range(n_step_pages)]
    return pl.pallas_call(
        functools.partial(_sb_sample_kernel, n_step_pages=n_step_pages, page=page, n_heads=n_heads,
                          single_step=n_steps == 1),
        grid_spec=pltpu.PrefetchScalarGridSpec(
            num_scalar_prefetch=1,
            grid=(b, n_steps),
            in_specs=[tok, tok, tok, _resident(bias_rows.shape, lambda i, s, pt: (0, 0))] + pages + pages,
            out_specs=tok,
            scratch_shapes=[pltpu.VMEM((rows, d), BF16), pltpu.VMEM((page, d), F32), pltpu.VMEM((page, d), F32),
                            pltpu.VMEM((rows, d), F32), pltpu.VMEM((rows, LANES), F32)]),
        out_shape=jax.ShapeDtypeStruct((b, n_q, d), F32),
        compiler_params=_params("parallel", "arbitrary"),
        name="sb_sample",
    )(page_table.reshape(-1), q, k_new, v_new, bias_rows,
      *([cache_k_t] * n_step_pages), *([cache_v_t] * n_step_pages))


HGRN_CHUNK = 64
HGRN_SUBCHUNKS = 4
HGRN_STAGE_LAG = 3
HGRN_DECAY_LIMIT = 80.0
HGRN_EXACT_CHUNK = 16
SAMPLE_PAD = 8
SAMPLE_ELEMS = 4
FINAL_TILE = 512
TAIL_GROUPS = 2
FFN_CHUNK = 1024
PROMPT_BLOCK = 256
PROMPT_HEADS = 16
PROMPT_STAGE_LAG = 2
STEP_PAGES = 16
SAMPLE_STAGE_LAG = 2


def _row_tile(t, want):
    return want if t % want == 0 else t


def _trunk(x, p, state0, past, w):
    b, l, d = x.shape
    t = b * l
    x2 = x.reshape(t, d)
    p2 = p.reshape(p.shape[0], t, p.shape[-1])
    q, fg, v, og = _hgrn_in(x2, w["g_mix"][0][None], w["a_lb"], w["w_a_in"], 0, _row_tile(t, 512))
    q, fg, v, og = (a.reshape(b, l, d) for a in (q, fg, v, og))
    if l % (HGRN_CHUNK * HGRN_SUBCHUNKS) == 0:
        o, s_fin = _hgrn_scan(q, fg, v, og, w["g_a_onorm"][0][None], state0, HGRN_CHUNK, HGRN_SUBCHUNKS, 1, BF16)
    else:
        pad = ((0, 0), (0, SAMPLE_PAD - l), (0, 0))
        n_elems = SAMPLE_ELEMS if b % SAMPLE_ELEMS == 0 else 1
        o, s_fin = _hgrn_scan(jnp.pad(q, pad), jnp.pad(fg, pad, constant_values=1.0), jnp.pad(v, pad),
                              jnp.pad(og, pad), w["g_a_onorm"][0][None], state0, SAMPLE_PAD, 1, n_elems, F32)
        o = o[:, :l]
    n_heads = d // SB_HEAD_DIM
    if past is None:
        tm = PROMPT_BLOCK
        h, k_t, v_t, q1, kb_t, vb_t = _layer_tail(x2, o.reshape(t, d), p2, 0, w, tm, final=False, seq_len=l)
        attn = _sb_prompt(q1.reshape(b, l, d), kb_t, vb_t, w["sb_bias"][0], PROMPT_BLOCK, PROMPT_HEADS)
        k, v_kv = (jnp.transpose(a.reshape(b, n_heads, SB_HEAD_DIM, l), (0, 3, 1, 2)) for a in (k_t, v_t))
    else:
        tm = _row_tile(t, 256)
        h, k, v_kv, q1 = _layer_tail(x2, o.reshape(t, d), p2, 0, w, tm, final=False, q_dtype=F32)
        cache_k, cache_v, page_table = past
        bias_rows = jnp.broadcast_to(jnp.tile(w["sb_bias"][0], l)[:, None], (l * n_heads, LANES))
        cache_k_t, cache_v_t = (jnp.transpose(c, (0, 2, 3, 1)).reshape(c.shape[0], d, c.shape[1])
                                for c in (cache_k, cache_v))
        attn = _sb_sample(q1.reshape(b, l, d), k.reshape(b, l, d), v_kv.reshape(b, l, d), bias_rows,
                          cache_k_t, cache_v_t, page_table, min(STEP_PAGES, page_table.shape[1]))
        k, v_kv = (a.reshape(b, l, n_heads, SB_HEAD_DIM) for a in (k, v_kv))
    (y,) = _layer_tail(h, attn.reshape(t, d), p2, 1, w, _row_tile(t, FINAL_TILE), final=True)
    return y.reshape(b, l, d), s_fin[None], k, v_kv


def kernel(x_prompt, x_sample, p_prompt, p_sample, state_hgrn, cache_k, cache_v, page_table, a_lb, w_a_in,
           g_a_onorm, w_a_o, g_kv, w_kv, w_b_q, w_b_o, sb_bias, g_mix, g_ffn, w_ffn_in, w_ffn_out, g_ple,
           w_ple_in, w_ple_gate, g_final):
    w = dict(a_lb=a_lb, g_a_onorm=g_a_onorm, g_kv=g_kv, sb_bias=sb_bias * LOG2E, g_mix=g_mix, g_ffn=g_ffn, g_ple=g_ple,
             g_final=g_final,
             w_a_in=w_a_in.astype(BF16), w_a_o=w_a_o.astype(BF16), w_kv=w_kv.astype(BF16),
             w_kv_t=w_kv.T.astype(BF16),
             w_b_q=w_b_q.astype(BF16), w_b_o=w_b_o.astype(BF16), w_ffn_in=w_ffn_in.astype(BF16),
             w_ffn_out=w_ffn_out.astype(BF16), w_ple_in=w_ple_in.astype(BF16),
             w_ple_gate=w_ple_gate.astype(BF16))
    y_p, st_p, k_p, v_p = _trunk(x_prompt, p_prompt, None, None, w)
    y_s, st_s, k_s, v_s = _trunk(x_sample, p_sample, state_hgrn[0], (cache_k, cache_v, page_table), w)
    return (y_p, y_s, st_p, st_s, k_p, v_p, k_s, v_s)
```

```python
import functools

import jax
import jax.numpy as jnp
from jax import lax
from jax.experimental import pallas as pl
from jax.experimental.pallas import tpu as pltpu

F32 = jnp.float32
BF16 = jnp.bfloat16
EPS = 1e-6
LOG2E = 1.4426950408889634
HGRN_HEAD_DIM = 128
SB_HEAD_DIM = 64
LANES = 128
VMEM_LIMIT_BYTES = 56 * 1024 * 1024


def _sigmoid(x):
    return 1.0 / (1.0 + jnp.exp(-x))


def _rms(x, g):
    return x * lax.rsqrt(jnp.mean(x * x, axis=-1, keepdims=True) + EPS) * g


def _dot(a, b):
    return jnp.dot(a, b, preferred_element_type=F32)


def _dot_nt(a, b):
    return lax.dot_general(a, b, (((1,), (1,)), ((), ())), preferred_element_type=F32)


def _dot_tn(a, b):
    return lax.dot_general(a, b, (((0,), (0,)), ((), ())), preferred_element_type=F32)


def _split3(x):
    hi = x.astype(BF16)
    r = x - hi.astype(F32)
    mid = r.astype(BF16)
    lo = (r - mid.astype(F32)).astype(BF16)
    return hi, mid, lo


def _params(*sem):
    return pltpu.CompilerParams(dimension_semantics=sem, vmem_limit_bytes=VMEM_LIMIT_BYTES)


def _resident(shape, index_map):
    return pl.BlockSpec(shape, index_map, pipeline_mode=pl.Buffered(1))


def _hgrn_in_kernel(x_ref, g_ref, alb_ref, w_ref, q_ref, fg_ref, v_ref, og_ref, *, layer):
    tm, d = x_ref.shape
    a = alb_ref[...]
    e = jnp.exp(a - jnp.max(a, axis=0, keepdims=True))
    lb = jnp.sum(e[0:layer + 1], axis=0, keepdims=True) / jnp.sum(e, axis=0, keepdims=True)
    n_groups = 2 if tm % 16 == 0 else 1
    groups = [slice(g * (tm // n_groups), (g + 1) * (tm // n_groups)) for g in range(n_groups)]
    hn = [_rms(x_ref[rs, :], g_ref[...]).astype(BF16) for rs in groups]
    for g, rs in enumerate(groups):
        pq = _dot(hn[g], w_ref[:, 0:d])
        q_ref[rs, :] = pq * _sigmoid(pq)
        pf = _dot(hn[g], w_ref[:, d:2 * d])
        fg_ref[rs, :] = lb + (1.0 - lb) * _sigmoid(pf)
        v_ref[rs, :] = _dot(hn[g], w_ref[:, 2 * d:3 * d])
        po = _dot(hn[g], w_ref[:, 3 * d:4 * d])
        og_ref[rs, :] = po * _sigmoid(po)


def _hgrn_in(x, g, a_lb, w_in, layer, tm):
    t, d = x.shape
    n_slots = a_lb.shape[0]
    row = pl.BlockSpec((tm, d), lambda i: (i, 0))
    out = jax.ShapeDtypeStruct((t, d), F32)
    return pl.pallas_call(
        functools.partial(_hgrn_in_kernel, layer=layer),
        grid=(t // tm,),
        in_specs=[row,
                  _resident((1, d), lambda i: (0, 0)),
                  _resident((n_slots, d), lambda i: (0, 0)),
                  _resident((None, d, 4 * d), lambda i: (layer, 0, 0))],
        out_specs=[row, row, row, row],
        out_shape=[out, out, out, out],
        compiler_params=_params("parallel"),
        name="hgrn_in",
    )(x, g, a_lb, w_in)


def _hgrn_chunk(refs, e, rows, size, exact):
    q_ref, fg_ref, v_ref, og_ref, gon_ref, o_ref, st_ref = refs
    hd = HGRN_HEAD_DIM
    ri = lax.broadcasted_iota(jnp.int32, (size, size), 0)
    ci = lax.broadcasted_iota(jnp.int32, (size, size), 1)
    causal = ri >= ci
    tri = jnp.where(causal, 1.0, 0.0).astype(BF16)
    fg = fg_ref[e, rows, :]
    kk = 1.0 - fg
    hi, mid, lo = _split3(jnp.log(fg))
    gcum = _dot(tri, hi) + _dot(tri, mid) + _dot(tri, lo)
    g_last = gcum[size - 1:size, :]
    q = q_ref[e, rows, :]
    q_dec = (q * jnp.exp(gcum)).astype(BF16)
    k_dec = (kk * jnp.exp(g_last - gcum)).astype(BF16)
    s_dec = jnp.exp(g_last)
    v = v_ref[e, rows, :].astype(BF16)
    og = og_ref[e, rows, :]
    gon = gon_ref[...]
    if exact:
        row = lax.broadcasted_iota(jnp.int32, gcum.shape, 0)
        pair_terms = [q * jnp.exp(jnp.where(row >= s, gcum - gcum[s:s + 1], 0.0)) * kk[s:s + 1] for s in range(size)]
    else:
        k_inv = (kk * jnp.exp(-gcum)).astype(BF16)

    def first_products(h, _):
        hs = slice(h * hd, (h + 1) * hd)
        st = st_ref[e, h]
        if exact:
            att = jnp.zeros((size, size), F32)
            for s, t in enumerate(pair_terms):
                att = att + jnp.where(ci == s, jnp.sum(t[:, hs], axis=-1, keepdims=True), 0.0)
        else:
            att = _dot_nt(q_dec[:, hs], k_inv[:, hs])
        inter = _dot_nt(q_dec[:, hs], st.astype(BF16))
        st_ref[e, h] = st * s_dec[:, hs] + _dot_tn(v[:, hs], k_dec[:, hs])
        return att, inter

    def outputs(h, prods):
        hs = slice(h * hd, (h + 1) * hd)
        att, inter = prods
        o = _dot(jnp.where(causal, att, 0.0).astype(BF16), v[:, hs]) + inter
        on = o * lax.rsqrt(jnp.mean(o * o, axis=-1, keepdims=True) + EPS)
        o_ref[e, rows, hs] = (on * gon[:, hs] * og[:, hs]).astype(o_ref.dtype)

    return (first_products, outputs), g_last


def _hgrn_scan_kernel(*refs, chunk, n_sub, has_init):
    if has_init:
        q_ref, fg_ref, v_ref, og_ref, gon_ref, s0_ref, o_ref, sout_ref, st_ref, st0_ref = refs
    else:
        q_ref, fg_ref, v_ref, og_ref, gon_ref, o_ref, sout_ref, st_ref, st0_ref = refs
        s0_ref = None
    n_elems, n_heads = st_ref.shape[0], st_ref.shape[1]
    c = pl.program_id(1)
    chunk_refs = (q_ref, fg_ref, v_ref, og_ref, gon_ref, o_ref, st_ref)

    @pl.when(c == 0)
    def _():
        if has_init:
            for e in range(n_elems):
                for h in range(n_heads):
                    st_ref[e, h] = s0_ref[e, h].T
        else:
            st_ref[...] = jnp.zeros(st_ref.shape, F32)

    st0_ref[...] = st_ref[...]
    stages, worst = [], None
    for e in range(n_elems):
        for s in range(n_sub):
            fns, g_last = _hgrn_chunk(chunk_refs, e, slice(s * chunk, (s + 1) * chunk), chunk, exact=False)
            worst = g_last if worst is None else jnp.minimum(worst, g_last)
            stages += [[functools.partial(fn, h) for fn in fns] for h in range(n_heads)]
    _emit_skewed(len(stages), [lambda i, st, k=k: stages[i][k](st) for k in range(2)], lag=HGRN_STAGE_LAG)

    @pl.when(jnp.min(worst) < -HGRN_DECAY_LIMIT)
    def _():
        st_ref[...] = st0_ref[...]
        small = min(HGRN_EXACT_CHUNK, chunk * n_sub)

        def redo(i, carry):
            rows = pl.ds(pl.multiple_of(i * small, small), small)
            for e in range(n_elems):
                fns, _ = _hgrn_chunk(chunk_refs, e, rows, small, exact=True)
                for h in range(n_heads):
                    fns[1](h, fns[0](h, None))
            return carry

        lax.fori_loop(0, chunk * n_sub // small, redo, 0)

    @pl.when(c == pl.num_programs(1) - 1)
    def _():
        for e in range(n_elems):
            for h in range(n_heads):
                sout_ref[e, h] = st_ref[e, h].T


def _hgrn_scan(q, fg, v, og, g_onorm, s0, chunk, n_sub, n_elems, out_dtype):
    b, l, d = q.shape
    n_heads = d // HGRN_HEAD_DIM
    rows = chunk * n_sub
    blk = pl.BlockSpec((n_elems, rows, d), lambda i, c: (i, c, 0))
    st_shape = (n_elems, n_heads, HGRN_HEAD_DIM, HGRN_HEAD_DIM)
    st_blk = pl.BlockSpec(st_shape, lambda i, c: (i, 0, 0, 0))
    in_specs = [blk, blk, blk, blk, _resident((1, d), lambda i, c: (0, 0))]
    args = [q, fg, v, og, g_onorm]
    if s0 is not None:
        in_specs.append(st_blk)
        args.append(s0)
    return pl.pallas_call(
        functools.partial(_hgrn_scan_kernel, chunk=chunk, n_sub=n_sub, has_init=s0 is not None),
        grid=(b // n_elems, l // rows),
        in_specs=in_specs,
        out_specs=[blk, st_blk],
        out_shape=[jax.ShapeDtypeStruct((b, l, d), out_dtype),
                   jax.ShapeDtypeStruct((b, n_heads, HGRN_HEAD_DIM, HGRN_HEAD_DIM), F32)],
        scratch_shapes=[pltpu.VMEM(st_shape, F32), pltpu.VMEM(st_shape, F32)],
        compiler_params=_params("parallel", "arbitrary"),
        name="hgrn_scan",
    )(*args)


def _layer_tail_kernel(*refs, d_ff, ff_chunk, final, n_groups):
    (h_ref, a_ref, p_ref, wo_ref, gffn_ref, win_ref, wout_ref, gple_ref, wpe_ref, wpg_ref) = refs[:10]
    tm, d = h_ref.shape
    size = tm // n_groups
    groups = [slice(g * size, (g + 1) * size) for g in range(n_groups)]
    st = [dict() for _ in groups]

    def mix(g):
        h1 = h_ref[groups[g], :] + _dot(a_ref[groups[g], :].astype(BF16), wo_ref[...])
        st[g].update(h=h1, hn=_rms(h1, gffn_ref[...]).astype(BF16))

    def ffn_up(g, c0, c1):
        gate = _dot(st[g]["hn"], win_ref[:, c0:c1])
        up = _dot(st[g]["hn"], win_ref[:, d_ff + c0:d_ff + c1])
        st[g]["act"] = (gate * _sigmoid(gate) * up).astype(BF16)

    def ffn_down(g, c0, c1):
        st[g]["h"] = st[g]["h"] + _dot(st[g]["act"], wout_ref[c0:c1, :])

    def ple_products(g):
        st[g]["pg"] = _dot(_rms(st[g]["h"], gple_ref[...]).astype(BF16), wpg_ref[...])
        st[g]["pe"] = _dot(p_ref[groups[g], :].astype(BF16), wpe_ref[...])

    def ple_add(g):
        h3 = st[g]["h"] + st[g]["pe"] * _sigmoid(st[g]["pg"])
        if final:
            y_ref[groups[g], :] = _rms(h3, gfin_ref[...])
        else:
            h_out_ref[groups[g], :] = h3
            st[g].update(hkv=_rms(h3, gkv_ref[...]).astype(BF16), hq=_rms(h3, gq_ref[...]).astype(BF16))

    def kv(g):
        if len(refs) > 18:
            if g == n_groups - 1:
                hkv = jnp.concatenate([s["hkv"] for s in st], axis=0)
                kt = _dot_nt(wkv_ref[0:d, :], hkv)
                vt = _dot_nt(wkv_ref[d:2 * d, :], hkv)
                k_ref[0] = kt
                v_ref[0] = vt
                kb_ref[0, 0] = kt.astype(BF16)
                vb_ref[0, 0] = vt.astype(BF16)
        else:
            k_ref[groups[g], :] = _dot(st[g]["hkv"], wkv_ref[:, 0:d])
            v_ref[groups[g], :] = _dot(st[g]["hkv"], wkv_ref[:, d:2 * d])

    def query(g):
        q_ref[groups[g], :] = (_dot(st[g]["hq"], wq_ref[...]) * (LOG2E * SB_HEAD_DIM ** -0.5)).astype(q_ref.dtype)

    stages = [mix]
    for c0 in range(0, d_ff, ff_chunk):
        c1 = min(c0 + ff_chunk, d_ff)
        stages += [functools.partial(ffn_up, c0=c0, c1=c1), functools.partial(ffn_down, c0=c0, c1=c1)]
    stages += [ple_products, ple_add]
    if final:
        gfin_ref, y_ref = refs[10:]
    else:
        gkv_ref, wkv_ref, gq_ref, wq_ref, h_out_ref, k_ref, v_ref, q_ref = refs[10:18]
        if len(refs) > 18:
            kb_ref, vb_ref = refs[18:]
        stages += [kv, query]
    for stage in stages:
        for g in range(n_groups):
            stage(g)


def _layer_tail(h, a, p, layer, w, tm, final, q_dtype=BF16, seq_len=None):
    t, d = h.shape
    ple = p.shape[-1]
    d_ff = w["w_ffn_out"].shape[1]
    row = pl.BlockSpec((tm, d), lambda i: (i, 0))
    vec = _resident((1, d), lambda i: (0, 0))

    def stacked(arr, idx):
        return _resident((None,) + arr.shape[1:], lambda i: (idx,) + (0,) * (arr.ndim - 1))

    j = layer - (w["w_ffn_in"].shape[0] - w["w_b_o"].shape[0])
    w_o = w["w_b_o"] if final else w["w_a_o"]
    in_specs = [row, row, pl.BlockSpec((None, tm, ple), lambda i: (layer, i, 0)),
                stacked(w_o, j if final else layer), vec, stacked(w["w_ffn_in"], layer),
                stacked(w["w_ffn_out"], layer), vec, stacked(w["w_ple_in"], layer), stacked(w["w_ple_gate"], layer)]
    args = [h, a, p, w_o, w["g_ffn"][layer][None], w["w_ffn_in"], w["w_ffn_out"], w["g_ple"][layer][None],
            w["w_ple_in"], w["w_ple_gate"]]
    if final:
        in_specs += [vec]
        args += [w["g_final"][None]]
        out_specs = [row]
        out_shape = [jax.ShapeDtypeStruct((t, d), F32)]
    else:
        w_kv = w["w_kv"] if seq_len is None else w["w_kv_t"]
        in_specs += [vec, _resident(w_kv.shape, lambda i: (0, 0)), vec, stacked(w["w_b_q"], 0)]
        args += [w["g_kv"][None], w_kv, w["g_mix"][layer + 1][None], w["w_b_q"]]
        if seq_len is None:
            out_specs = [row, row, row, row]
            out_shape = [jax.ShapeDtypeStruct((t, d), F32)] * 3 + [jax.ShapeDtypeStruct((t, d), q_dtype)]
        else:
            nb, nc = t // seq_len, seq_len // tm
            kv_t = pl.BlockSpec((1, d, tm), lambda i: (i // nc, 0, i % nc))
            kv_blocks = pl.BlockSpec((1, 1, d, tm), lambda i: (i // nc, i % nc, 0, 0))
            out_specs = [row, kv_t, kv_t, row, kv_blocks, kv_blocks]
            out_shape = ([jax.ShapeDtypeStruct((t, d), F32)] + [jax.ShapeDtypeStruct((nb, d, seq_len), F32)] * 2
                         + [jax.ShapeDtypeStruct((t, d), q_dtype)]
                         + [jax.ShapeDtypeStruct((nb, nc, d, tm), BF16)] * 2)
    return pl.pallas_call(
        functools.partial(_layer_tail_kernel, d_ff=d_ff, ff_chunk=FFN_CHUNK, final=final,
                          n_groups=TAIL_GROUPS if tm % (TAIL_GROUPS * LANES) == 0 else 1),
        grid=(t // tm,),
        in_specs=in_specs,
        out_specs=out_specs,
        out_shape=out_shape,
        compiler_params=_params("parallel"),
        name="layer_tail_final" if final else "layer_tail_kvq",
    )(*args)


def _strict_upper(n):
    ji = lax.broadcasted_iota(jnp.int32, (n, n), 0)
    si = lax.broadcasted_iota(jnp.int32, (n, n), 1)
    return jnp.where(ji > si, 1.0, 0.0).astype(BF16)


def _sb_logs(z, visible):
    m = jnp.minimum(z, 0.0)
    d = m - z
    l1p = jnp.log2(1.0 + jnp.exp2(m + d))
    log_1mb = d - l1p
    if visible is not None:
        log_1mb = jnp.where(visible, log_1mb, 0.0)
    return m - l1p, log_1mb


def _sb_weights(log_b, log_1mb, tail, carry, visible):
    carry_b = jnp.concatenate([carry] * (log_b.shape[1] // LANES), axis=1)
    a = jnp.exp2(log_b + tail + carry_b)
    if visible is not None:
        a = jnp.where(visible, a, 0.0)
    return a, carry + (tail[:, 0:1] + log_1mb[:, 0:1])


def _sb_block(z, upper, carry, visible):
    log_b, log_1mb = _sb_logs(z, visible)
    tail = _dot(log_1mb.astype(BF16), upper)
    return _sb_weights(log_b, log_1mb, tail, carry, visible)


def _emit_skewed(n_items, stages, lag=1):
    state = [None] * n_items
    for t in range(n_items + (len(stages) - 1) * lag):
        for s in reversed(range(len(stages))):
            i = t - s * lag
            if 0 <= i < n_items:
                state[i] = stages[s](i, state[i])


def _sb_prompt_kernel(bias_ref, q_ref, k_ref, v_ref, o_ref, qh_ref, acc_ref, car_ref, *, blk, n_heads):
    hg = pl.program_id(1)
    qi = pl.program_id(2)

    lane = lax.broadcasted_iota(jnp.int32, (blk, LANES), 1)
    first = lane < SB_HEAD_DIM
    for j in range(n_heads):
        qt = q_ref[0, :, (j // 2) * LANES:(j // 2 + 1) * LANES]
        qh_ref[j, :, 0:LANES] = jnp.where(first if j % 2 == 0 else ~first, qt, jnp.zeros_like(qt))
        bias = jnp.full((blk, LANES), bias_ref[n_heads * hg + j], F32)
        hi = bias.astype(BF16).astype(F32)
        qh_ref[j, :, LANES:2 * LANES] = jnp.where(lane == 0, hi, jnp.where(lane == 1, bias - hi, 0.0)).astype(BF16)
    ones_rows = jnp.where(lax.broadcasted_iota(jnp.int32, (LANES, blk), 0) < 2, 1.0, 0.0).astype(BF16)
    upper = _strict_upper(blk)
    ti = lax.broadcasted_iota(jnp.int32, (blk, blk), 0)
    si = lax.broadcasted_iota(jnp.int32, (blk, blk), 1)
    strictly_before = si < ti
    acc_ref[...] = jnp.zeros(acc_ref.shape, F32)
    car_ref[...] = jnp.zeros(car_ref.shape, F32)

    def process(kb, visible):

        def tile(j):
            return slice((j // 2) * LANES, (j // 2 + 1) * LANES)

        def logs(j, _):
            z = _dot(qh_ref[j], jnp.concatenate([k_ref[0, kb, tile(j), :], ones_rows], axis=0))
            return _sb_logs(z, visible)

        def tails(j, st):
            return st + (_dot(st[1].astype(BF16), upper),)

        def weights(j, st):
            a, car_ref[j] = _sb_weights(*st, car_ref[j], visible)
            acc_ref[j] += _dot_nt(a.astype(BF16), v_ref[0, kb, tile(j), :])

        _emit_skewed(n_heads, [logs, tails, weights], lag=PROMPT_STAGE_LAG)

    process(qi, strictly_before)

    def body(i, carry):
        process(qi - 1 - i, None)
        return carry

    lax.fori_loop(0, qi, body, 0)
    for t in range(n_heads // 2):
        o_ref[0, :, t * LANES:(t + 1) * LANES] = jnp.where(first, acc_ref[2 * t], acc_ref[2 * t + 1]).astype(o_ref.dtype)


def _sb_prompt(q, k_t, v_t, bias, blk, n_heads):
    b, l, d = q.shape
    width = n_heads * SB_HEAD_DIM
    q_blk = pl.BlockSpec((1, blk, width), lambda i, hg, qi, bias: (i, qi, hg))
    kv_blk = pl.BlockSpec((1, l // blk, width, blk), lambda i, hg, qi, bias: (i, 0, hg, 0))
    return pl.pallas_call(
        functools.partial(_sb_prompt_kernel, blk=blk, n_heads=n_heads),
        grid_spec=pltpu.PrefetchScalarGridSpec(
            num_scalar_prefetch=1,
            grid=(b, d // width, l // blk),
            in_specs=[q_blk, kv_blk, kv_blk],
            out_specs=q_blk,
            scratch_shapes=[pltpu.VMEM((n_heads, blk, 2 * LANES), BF16),
                            pltpu.VMEM((n_heads, blk, LANES), F32), pltpu.VMEM((n_heads, blk, LANES), F32)]),
        out_shape=jax.ShapeDtypeStruct((b, l, d), BF16),
        compiler_params=_params("parallel", "parallel", "arbitrary"),
        name="sb_prompt",
    )(bias, q, k_t, v_t)


def _sb_sample_kernel(pt_ref, q_ref, kn_ref, vn_ref, bias_ref, *refs, n_step_pages, page, n_heads, single_step):
    k_refs = refs[:n_step_pages]
    v_refs = refs[n_step_pages:2 * n_step_pages]
    o_ref, qbd_ref, kn_buf, vn_buf, acc_ref, car_ref = refs[2 * n_step_pages:]
    del pt_ref
    step = pl.program_id(1)
    n_q, d = q_ref.shape[1], q_ref.shape[2]
    rows = n_q * n_heads
    upper = _strict_upper(page)
    bias = bias_ref[...]
    hrow = lax.broadcasted_iota(jnp.int32, (n_heads, d), 0)
    hlane = lax.broadcasted_iota(jnp.int32, (n_heads, d), 1) // SB_HEAD_DIM
    own_head = hrow == hlane

    def run(blocks):
        def logs(i, _):
            return _sb_logs(blocks[i][0](qbd_ref[...]) + bias, blocks[i][2])

        def tails(i, st):
            return st + (_dot(st[1].astype(BF16), upper),)

        def weights(i, st):
            a, car_ref[...] = _sb_weights(*st, car_ref[...], blocks[i][2])
            acc_ref[...] += blocks[i][1](a.astype(BF16))

        _emit_skewed(len(blocks), [logs, tails, weights], lag=SAMPLE_STAGE_LAG)

    def start():
        q = q_ref[0]
        for qi in range(n_q):
            qrow = jnp.broadcast_to(q[qi:qi + 1, :], (n_heads, d))
            qbd_ref[qi * n_heads:(qi + 1) * n_heads, :] = jnp.where(own_head, qrow, 0.0).astype(BF16)
        acc_ref[...] = jnp.zeros(acc_ref.shape, F32)
        car_ref[...] = jnp.zeros(car_ref.shape, F32)
        kn_buf[...] = jnp.zeros(kn_buf.shape, F32)
        vn_buf[...] = jnp.zeros(vn_buf.shape, F32)
        kn_buf[0:n_q, :] = kn_ref[0]
        vn_buf[0:n_q, :] = vn_ref[0]

    r_q = lax.broadcasted_iota(jnp.int32, (rows, page), 0) // n_heads
    s_k = lax.broadcasted_iota(jnp.int32, (rows, page), 1)
    new_block = (lambda qbd: _dot_nt(qbd, kn_buf[...].astype(BF16)),
                 lambda a: _dot(a, vn_buf[...].astype(BF16)), s_k < r_q)
    pages = [(lambda qbd, r=kr: _dot(qbd, r[0].astype(BF16)), lambda a, r=vr: _dot_nt(a, r[0].astype(BF16)), None)
             for kr, vr in zip(k_refs, v_refs)]
    if single_step:
        start()
        run([new_block] + pages)
    else:
        @pl.when(step == 0)
        def _():
            start()
            run([new_block])

        run(pages)

    @pl.when(step == pl.num_programs(1) - 1)
    def _():
        out_rows = []
        for qi in range(n_q):
            blk = acc_ref[qi * n_heads:(qi + 1) * n_heads, :]
            out_rows.append(jnp.sum(jnp.where(own_head, blk, 0.0), axis=0, keepdims=True))
        o_ref[0] = jnp.concatenate(out_rows, axis=0)


def _sb_sample(q, k_new, v_new, bias_rows, cache_k_t, cache_v_t, page_table, n_step_pages):
    b, n_q, d = q.shape
    n_pages = page_table.shape[1]
    page = cache_k_t.shape[2]
    n_heads = d // SB_HEAD_DIM
    rows = n_q * n_heads
    n_steps = n_pages // n_step_pages
    tok = pl.BlockSpec((1, n_q, d), lambda i, s, pt: (i, 0, 0))

    def page_spec(j):
        return pl.BlockSpec((1, d, page),
                            lambda i, s, pt: (pt[i * n_pages + (n_pages - 1 - (s * n_step_pages + j))], 0, 0))

    pages = [page_spec(j) for j in range(n_step_pages)]
    return pl.pallas_call(
        functools.partial(_sb_sample_kernel, n_step_pages=n_step_pages, page=page, n_heads=n_heads,
                          single_step=n_steps == 1),
        grid_spec=pltpu.PrefetchScalarGridSpec(
            num_scalar_prefetch=1,
            grid=(b, n_steps),
            in_specs=[tok, tok, tok, _resident(bias_rows.shape, lambda i, s, pt: (0, 0))] + pages + pages,
            out_specs=tok,
            scratch_shapes=[pltpu.VMEM((rows, d), BF16), pltpu.VMEM((page, d), F32), pltpu.VMEM((page, d), F32),
                            pltpu.VMEM((rows, d), F32), pltpu.VMEM((rows, LANES), F32)]),
        out_shape=jax.ShapeDtypeStruct((b, n_q, d), F32),
        compiler_params=_params("parallel", "arbitrary"),
        name="sb_sample",
    )(page_table.reshape(-1), q, k_new, v_new, bias_rows,
      *([cache_k_t] * n_step_pages), *([cache_v_t] * n_step_pages))


HGRN_CHUNK = 64
HGRN_SUBCHUNKS = 16
HGRN_STAGE_LAG = 5
HGRN_DECAY_LIMIT = 80.0
HGRN_EXACT_CHUNK = 16
SAMPLE_PAD = 8
SAMPLE_ELEMS = 8
FINAL_TILE = 512
TAIL_GROUPS = 2
FFN_CHUNK = 1024
PROMPT_BLOCK = 256
PROMPT_HEADS = 16
PROMPT_STAGE_LAG = 2
STEP_PAGES = 16
SAMPLE_STAGE_LAG = 3


def _row_tile(t, want):
    return want if t % want == 0 else t


def _trunk(x, p, state0, past, w):
    b, l, d = x.shape
    t = b * l
    x2 = x.reshape(t, d)
    p2 = p.reshape(p.shape[0], t, p.shape[-1])
    q, fg, v, og = _hgrn_in(x2, w["g_mix"][0][None], w["a_lb"], w["w_a_in"], 0, _row_tile(t, 512))
    q, fg, v, og = (a.reshape(b, l, d) for a in (q, fg, v, og))
    if l % HGRN_CHUNK == 0:
        n_sub = HGRN_SUBCHUNKS
        while (l // HGRN_CHUNK) % n_sub:
            n_sub //= 2
        o, s_fin = _hgrn_scan(q, fg, v, og, w["g_a_onorm"][0][None], state0, HGRN_CHUNK, n_sub, 1, BF16)
    else:
        pad = ((0, 0), (0, SAMPLE_PAD - l), (0, 0))
        n_elems = SAMPLE_ELEMS if b % SAMPLE_ELEMS == 0 else 1
        o, s_fin = _hgrn_scan(jnp.pad(q, pad), jnp.pad(fg, pad, constant_values=1.0), jnp.pad(v, pad),
                              jnp.pad(og, pad), w["g_a_onorm"][0][None], state0, SAMPLE_PAD, 1, n_elems, F32)
        o = o[:, :l]
    n_heads = d // SB_HEAD_DIM
    if past is None:
        tm = PROMPT_BLOCK
        h, k_t, v_t, q1, kb_t, vb_t = _layer_tail(x2, o.reshape(t, d), p2, 0, w, tm, final=False, seq_len=l)
        attn = _sb_prompt(q1.reshape(b, l, d), kb_t, vb_t, w["sb_bias"][0], PROMPT_BLOCK, PROMPT_HEADS)
        k, v_kv = (jnp.transpose(a.reshape(b, n_heads, SB_HEAD_DIM, l), (0, 3, 1, 2)) for a in (k_t, v_t))
    else:
        tm = _row_tile(t, 256)
        h, k, v_kv, q1 = _layer_tail(x2, o.reshape(t, d), p2, 0, w, tm, final=False, q_dtype=F32)
        cache_k, cache_v, page_table = past
        bias_rows = jnp.broadcast_to(jnp.tile(w["sb_bias"][0], l)[:, None], (l * n_heads, LANES))
        cache_k_t, cache_v_t = (jnp.transpose(c, (0, 2, 3, 1)).reshape(c.shape[0], d, c.shape[1])
                                for c in (cache_k, cache_v))
        attn = _sb_sample(q1.reshape(b, l, d), k.reshape(b, l, d), v_kv.reshape(b, l, d), bias_rows,
                          cache_k_t, cache_v_t, page_table, min(STEP_PAGES, page_table.shape[1]))
        k, v_kv = (a.reshape(b, l, n_heads, SB_HEAD_DIM) for a in (k, v_kv))
    (y,) = _layer_tail(h, attn.reshape(t, d), p2, 1, w, _row_tile(t, FINAL_TILE), final=True)
    return y.reshape(b, l, d), s_fin[None], k, v_kv


def kernel(x_prompt, x_sample, p_prompt, p_sample, state_hgrn, cache_k, cache_v, page_table, a_lb, w_a_in,
           g_a_onorm, w_a_o, g_kv, w_kv, w_b_q, w_b_o, sb_bias, g_mix, g_ffn, w_ffn_in, w_ffn_out, g_ple,
           w_ple_in, w_ple_gate, g_final):
    w = dict(a_lb=a_lb, g_a_onorm=g_a_onorm, g_kv=g_kv, sb_bias=sb_bias * LOG2E, g_mix=g_mix, g_ffn=g_ffn, g_ple=g_ple,
             g_final=g_final,
             w_a_in=w_a_in.astype(BF16), w_a_o=w_a_o.astype(BF16), w_kv=w_kv.astype(BF16),
             w_kv_t=w_kv.T.astype(BF16),
             w_b_q=w_b_q.astype(BF16), w_b_o=w_b_o.astype(BF16), w_ffn_in=w_ffn_in.astype(BF16),
             w_ffn_out=w_ffn_out.astype(BF16), w_ple_in=w_ple_in.astype(BF16),
             w_ple_gate=w_ple_gate.astype(BF16))
    y_p, st_p, k_p, v_p = _trunk(x_prompt, p_prompt, None, None, w)
    y_s, st_s, k_s, v_s = _trunk(x_sample, p_sample, state_hgrn[0], (cache_k, cache_v, page_table), w)
    return (y_p, y_s, st_p, st_s, k_p, v_p, k_s, v_s)
```

```python
import functools

import jax
import jax.numpy as jnp
from jax import lax
from jax.experimental import pallas as pl
from jax.experimental.pallas import tpu as pltpu

F32 = jnp.float32
BF16 = jnp.bfloat16
EPS = 1e-6
LOG2E = 1.4426950408889634
HGRN_HEAD_DIM = 128
SB_HEAD_DIM = 64
LANES = 128
VMEM_LIMIT_BYTES = 56 * 1024 * 1024


def _sigmoid(x):
    return 1.0 / (1.0 + jnp.exp(-x))


def _rms(x, g):
    return x * lax.rsqrt(jnp.mean(x * x, axis=-1, keepdims=True) + EPS) * g


def _dot(a, b):
    return jnp.dot(a, b, preferred_element_type=F32)


def _dot_nt(a, b):
    return lax.dot_general(a, b, (((1,), (1,)), ((), ())), preferred_element_type=F32)


def _dot_tn(a, b):
    return lax.dot_general(a, b, (((0,), (0,)), ((), ())), preferred_element_type=F32)


def _split3(x):
    hi = x.astype(BF16)
    r = x - hi.astype(F32)
    mid = r.astype(BF16)
    lo = (r - mid.astype(F32)).astype(BF16)
    return hi, mid, lo


def _params(*sem):
    return pltpu.CompilerParams(dimension_semantics=sem, vmem_limit_bytes=VMEM_LIMIT_BYTES)


def _resident(shape, index_map):
    return pl.BlockSpec(shape, index_map, pipeline_mode=pl.Buffered(1))


def _hgrn_in_kernel(x_ref, g_ref, alb_ref, w_ref, q_ref, fg_ref, v_ref, og_ref, *, layer):
    tm, d = x_ref.shape
    a = alb_ref[...]
    e = jnp.exp(a - jnp.max(a, axis=0, keepdims=True))
    lb = jnp.sum(e[0:layer + 1], axis=0, keepdims=True) / jnp.sum(e, axis=0, keepdims=True)
    n_groups = 2 if tm % 16 == 0 else 1
    groups = [slice(g * (tm // n_groups), (g + 1) * (tm // n_groups)) for g in range(n_groups)]
    hn = [_rms(x_ref[rs, :], g_ref[...]).astype(BF16) for rs in groups]
    for g, rs in enumerate(groups):
        pq = _dot(hn[g], w_ref[:, 0:d])
        q_ref[rs, :] = pq * _sigmoid(pq)
        pf = _dot(hn[g], w_ref[:, d:2 * d])
        fg_ref[rs, :] = lb + (1.0 - lb) * _sigmoid(pf)
        v_ref[rs, :] = _dot(hn[g], w_ref[:, 2 * d:3 * d])
        po = _dot(hn[g], w_ref[:, 3 * d:4 * d])
        og_ref[rs, :] = po * _sigmoid(po)


def _hgrn_in(x, g, a_lb, w_in, layer, tm):
    t, d = x.shape
    n_slots = a_lb.shape[0]
    row = pl.BlockSpec((tm, d), lambda i: (i, 0))
    out = jax.ShapeDtypeStruct((t, d), F32)
    return pl.pallas_call(
        functools.partial(_hgrn_in_kernel, layer=layer),
        grid=(t // tm,),
        in_specs=[row,
                  _resident((1, d), lambda i: (0, 0)),
                  _resident((n_slots, d), lambda i: (0, 0)),
                  _resident((None, d, 4 * d), lambda i: (layer, 0, 0))],
        out_specs=[row, row, row, row],
        out_shape=[out, out, out, out],
        compiler_params=_params("parallel"),
        name="hgrn_in",
    )(x, g, a_lb, w_in)


def _hgrn_chunk(refs, e, rows, size, exact):
    q_ref, fg_ref, v_ref, og_ref, gon_ref, o_ref, st_ref = refs
    hd = HGRN_HEAD_DIM
    ri = lax.broadcasted_iota(jnp.int32, (size, size), 0)
    ci = lax.broadcasted_iota(jnp.int32, (size, size), 1)
    causal = ri >= ci
    tri = jnp.where(causal, 1.0, 0.0).astype(BF16)
    fg = fg_ref[e, rows, :]
    kk = 1.0 - fg
    hi, mid, lo = _split3(jnp.log(fg))
    gcum = _dot(tri, hi) + _dot(tri, mid) + _dot(tri, lo)
    g_last = gcum[size - 1:size, :]
    q = q_ref[e, rows, :]
    q_dec = (q * jnp.exp(gcum)).astype(BF16)
    k_dec = (kk * jnp.exp(g_last - gcum)).astype(BF16)
    s_dec = jnp.exp(g_last)
    v = v_ref[e, rows, :].astype(BF16)
    og = og_ref[e, rows, :]
    gon = gon_ref[...]
    if exact:
        row = lax.broadcasted_iota(jnp.int32, gcum.shape, 0)
        pair_terms = [q * jnp.exp(jnp.where(row >= s, gcum - gcum[s:s + 1], 0.0)) * kk[s:s + 1] for s in range(size)]
    else:
        k_inv = (kk * jnp.exp(-gcum)).astype(BF16)

    def first_products(h, _):
        hs = slice(h * hd, (h + 1) * hd)
        st = st_ref[e, h]
        if exact:
            att = jnp.zeros((size, size), F32)
            for s, t in enumerate(pair_terms):
                att = att + jnp.where(ci == s, jnp.sum(t[:, hs], axis=-1, keepdims=True), 0.0)
        else:
            att = _dot_nt(q_dec[:, hs], k_inv[:, hs])
        inter = _dot_nt(q_dec[:, hs], st.astype(BF16))
        st_ref[e, h] = st * s_dec[:, hs] + _dot_tn(v[:, hs], k_dec[:, hs])
        return att, inter

    def outputs(h, prods):
        hs = slice(h * hd, (h + 1) * hd)
        att, inter = prods
        o = _dot(jnp.where(causal, att, 0.0).astype(BF16), v[:, hs]) + inter
        on = o * lax.rsqrt(jnp.mean(o * o, axis=-1, keepdims=True) + EPS)
        o_ref[e, rows, hs] = (on * gon[:, hs] * og[:, hs]).astype(o_ref.dtype)

    return (first_products, outputs), g_last


def _hgrn_scan_kernel(*refs, chunk, n_sub, has_init):
    if has_init:
        q_ref, fg_ref, v_ref, og_ref, gon_ref, s0_ref, o_ref, sout_ref, st_ref, st0_ref = refs
    else:
        q_ref, fg_ref, v_ref, og_ref, gon_ref, o_ref, sout_ref, st_ref, st0_ref = refs
        s0_ref = None
    n_elems, n_heads = st_ref.shape[0], st_ref.shape[1]
    c = pl.program_id(1)
    chunk_refs = (q_ref, fg_ref, v_ref, og_ref, gon_ref, o_ref, st_ref)

    @pl.when(c == 0)
    def _():
        if has_init:
            for e in range(n_elems):
                for h in range(n_heads):
                    st_ref[e, h] = s0_ref[e, h].T
        else:
            st_ref[...] = jnp.zeros(st_ref.shape, F32)

    st0_ref[...] = st_ref[...]
    stages, worst = [], None
    for e in range(n_elems):
        for s in range(n_sub):
            fns, g_last = _hgrn_chunk(chunk_refs, e, slice(s * chunk, (s + 1) * chunk), chunk, exact=False)
            worst = g_last if worst is None else jnp.minimum(worst, g_last)
            stages += [[functools.partial(fn, h) for fn in fns] for h in range(n_heads)]
    _emit_skewed(len(stages), [lambda i, st, k=k: stages[i][k](st) for k in range(2)], lag=HGRN_STAGE_LAG)

    @pl.when(jnp.min(worst) < -HGRN_DECAY_LIMIT)
    def _():
        st_ref[...] = st0_ref[...]
        small = min(HGRN_EXACT_CHUNK, chunk * n_sub)

        def redo(i, carry):
            rows = pl.ds(pl.multiple_of(i * small, small), small)
            for e in range(n_elems):
                fns, _ = _hgrn_chunk(chunk_refs, e, rows, small, exact=True)
                for h in range(n_heads):
                    fns[1](h, fns[0](h, None))
            return carry

        lax.fori_loop(0, chunk * n_sub // small, redo, 0)

    @pl.when(c == pl.num_programs(1) - 1)
    def _():
        for e in range(n_elems):
            for h in range(n_heads):
                sout_ref[e, h] = st_ref[e, h].T


def _hgrn_scan(q, fg, v, og, g_onorm, s0, chunk, n_sub, n_elems, out_dtype):
    b, l, d = q.shape
    n_heads = d // HGRN_HEAD_DIM
    rows = chunk * n_sub
    blk = pl.BlockSpec((n_elems, rows, d), lambda i, c: (i, c, 0))
    st_shape = (n_elems, n_heads, HGRN_HEAD_DIM, HGRN_HEAD_DIM)
    st_blk = pl.BlockSpec(st_shape, lambda i, c: (i, 0, 0, 0))
    in_specs = [blk, blk, blk, blk, _resident((1, d), lambda i, c: (0, 0))]
    args = [q, fg, v, og, g_onorm]
    if s0 is not None:
        in_specs.append(st_blk)
        args.append(s0)
    return pl.pallas_call(
        functools.partial(_hgrn_scan_kernel, chunk=chunk, n_sub=n_sub, has_init=s0 is not None),
        grid=(b // n_elems, l // rows),
        in_specs=in_specs,
        out_specs=[blk, st_blk],
        out_shape=[jax.ShapeDtypeStruct((b, l, d), out_dtype),
                   jax.ShapeDtypeStruct((b, n_heads, HGRN_HEAD_DIM, HGRN_HEAD_DIM), F32)],
        scratch_shapes=[pltpu.VMEM(st_shape, F32), pltpu.VMEM(st_shape, F32)],
        compiler_params=_params("parallel", "arbitrary"),
        name="hgrn_scan",
    )(*args)


def _layer_tail_kernel(*refs, d_ff, ff_chunk, final, n_groups):
    (h_ref, a_ref, p_ref, wo_ref, gffn_ref, win_ref, wout_ref, gple_ref, wpe_ref, wpg_ref) = refs[:10]
    tm, d = h_ref.shape
    size = tm // n_groups
    groups = [slice(g * size, (g + 1) * size) for g in range(n_groups)]
    st = [dict() for _ in groups]

    def mix(g):
        h1 = h_ref[groups[g], :] + _dot(a_ref[groups[g], :].astype(BF16), wo_ref[...])
        st[g].update(h=h1, hn=_rms(h1, gffn_ref[...]).astype(BF16))

    def ffn_up(g, c0, c1):
        gate = _dot(st[g]["hn"], win_ref[:, c0:c1])
        up = _dot(st[g]["hn"], win_ref[:, d_ff + c0:d_ff + c1])
        st[g]["act"] = (gate * _sigmoid(gate) * up).astype(BF16)

    def ffn_down(g, c0, c1):
        st[g]["h"] = st[g]["h"] + _dot(st[g]["act"], wout_ref[c0:c1, :])

    def ple_products(g):
        st[g]["pg"] = _dot(_rms(st[g]["h"], gple_ref[...]).astype(BF16), wpg_ref[...])
        st[g]["pe"] = _dot(p_ref[groups[g], :].astype(BF16), wpe_ref[...])

    def ple_add(g):
        h3 = st[g]["h"] + st[g]["pe"] * _sigmoid(st[g]["pg"])
        if final:
            y_ref[groups[g], :] = _rms(h3, gfin_ref[...])
        else:
            h_out_ref[groups[g], :] = h3
            st[g].update(hkv=_rms(h3, gkv_ref[...]).astype(BF16), hq=_rms(h3, gq_ref[...]).astype(BF16))

    def kv(g):
        if len(refs) > 18:
            if g == n_groups - 1:
                hkv = jnp.concatenate([s["hkv"] for s in st], axis=0)
                kt = _dot_nt(wkv_ref[0:d, :], hkv)
                vt = _dot_nt(wkv_ref[d:2 * d, :], hkv)
                k_ref[0] = kt
                v_ref[0] = vt
                kb_ref[0, 0] = kt.astype(BF16)
                vb_ref[0, 0] = vt.astype(BF16)
        else:
            k_ref[groups[g], :] = _dot(st[g]["hkv"], wkv_ref[:, 0:d])
            v_ref[groups[g], :] = _dot(st[g]["hkv"], wkv_ref[:, d:2 * d])

    def query(g):
        q_ref[groups[g], :] = (_dot(st[g]["hq"], wq_ref[...]) * (LOG2E * SB_HEAD_DIM ** -0.5)).astype(q_ref.dtype)

    stages = [mix]
    for c0 in range(0, d_ff, ff_chunk):
        c1 = min(c0 + ff_chunk, d_ff)
        stages += [functools.partial(ffn_up, c0=c0, c1=c1), functools.partial(ffn_down, c0=c0, c1=c1)]
    stages += [ple_products, ple_add]
    if final:
        gfin_ref, y_ref = refs[10:]
    else:
        gkv_ref, wkv_ref, gq_ref, wq_ref, h_out_ref, k_ref, v_ref, q_ref = refs[10:18]
        if len(refs) > 18:
            kb_ref, vb_ref = refs[18:]
        stages += [kv, query]
    for stage in stages:
        for g in range(n_groups):
            stage(g)


def _layer_tail(h, a, p, layer, w, tm, final, q_dtype=BF16, seq_len=None):
    t, d = h.shape
    ple = p.shape[-1]
    d_ff = w["w_ffn_out"].shape[1]
    row = pl.BlockSpec((tm, d), lambda i: (i, 0))
    vec = _resident((1, d), lambda i: (0, 0))

    def stacked(arr, idx):
        return _resident((None,) + arr.shape[1:], lambda i: (idx,) + (0,) * (arr.ndim - 1))

    j = layer - (w["w_ffn_in"].shape[0] - w["w_b_o"].shape[0])
    w_o = w["w_b_o"] if final else w["w_a_o"]
    in_specs = [row, row, pl.BlockSpec((None, tm, ple), lambda i: (layer, i, 0)),
                stacked(w_o, j if final else layer), vec, stacked(w["w_ffn_in"], layer),
                stacked(w["w_ffn_out"], layer), vec, stacked(w["w_ple_in"], layer), stacked(w["w_ple_gate"], layer)]
    args = [h, a, p, w_o, w["g_ffn"][layer][None], w["w_ffn_in"], w["w_ffn_out"], w["g_ple"][layer][None],
            w["w_ple_in"], w["w_ple_gate"]]
    if final:
        in_specs += [vec]
        args += [w["g_final"][None]]
        out_specs = [row]
        out_shape = [jax.ShapeDtypeStruct((t, d), F32)]
    else:
        w_kv = w["w_kv"] if seq_len is None else w["w_kv_t"]
        in_specs += [vec, _resident(w_kv.shape, lambda i: (0, 0)), vec, stacked(w["w_b_q"], 0)]
        args += [w["g_kv"][None], w_kv, w["g_mix"][layer + 1][None], w["w_b_q"]]
        if seq_len is None:
            out_specs = [row, row, row, row]
            out_shape = [jax.ShapeDtypeStruct((t, d), F32)] * 3 + [jax.ShapeDtypeStruct((t, d), q_dtype)]
        else:
            nb, nc = t // seq_len, seq_len // tm
            kv_t = pl.BlockSpec((1, d, tm), lambda i: (i // nc, 0, i % nc))
            kv_blocks = pl.BlockSpec((1, 1, d, tm), lambda i: (i // nc, i % nc, 0, 0))
            out_specs = [row, kv_t, kv_t, row, kv_blocks, kv_blocks]
            out_shape = ([jax.ShapeDtypeStruct((t, d), F32)] + [jax.ShapeDtypeStruct((nb, d, seq_len), F32)] * 2
                         + [jax.ShapeDtypeStruct((t, d), q_dtype)]
                         + [jax.ShapeDtypeStruct((nb, nc, d, tm), BF16)] * 2)
    return pl.pallas_call(
        functools.partial(_layer_tail_kernel, d_ff=d_ff, ff_chunk=FFN_CHUNK, final=final,
                          n_groups=TAIL_GROUPS if tm % (TAIL_GROUPS * LANES) == 0 else 1),
        grid=(t // tm,),
        in_specs=in_specs,
        out_specs=out_specs,
        out_shape=out_shape,
        compiler_params=_params("parallel"),
        name="layer_tail_final" if final else "layer_tail_kvq",
    )(*args)


def _strict_upper(n):
    ji = lax.broadcasted_iota(jnp.int32, (n, n), 0)
    si = lax.broadcasted_iota(jnp.int32, (n, n), 1)
    return jnp.where(ji >= si, 1.0, 0.0).astype(BF16)


def _sb_logs(z, visible):
    m = jnp.minimum(z, 0.0)
    d = m - z
    log_1mb = d - jnp.log2(1.0 + jnp.exp2(m + d))
    if visible is not None:
        log_1mb = jnp.where(visible, log_1mb, 0.0)
    return z, log_1mb


def _sb_weights(z, log_1mb, tail, carry, visible):
    del log_1mb
    carry_b = jnp.concatenate([carry] * (z.shape[1] // LANES), axis=1)
    a = jnp.exp2(z + tail + carry_b)
    if visible is not None:
        a = jnp.where(visible, a, 0.0)
    return a, carry + tail[:, 0:1]


def _sb_block(z, upper, carry, visible):
    z, log_1mb = _sb_logs(z, visible)
    tail = _dot(log_1mb.astype(BF16), upper)
    return _sb_weights(z, log_1mb, tail, carry, visible)


def _emit_skewed(n_items, stages, lag=1):
    state = [None] * n_items
    for t in range(n_items + (len(stages) - 1) * lag):
        for s in reversed(range(len(stages))):
            i = t - s * lag
            if 0 <= i < n_items:
                state[i] = stages[s](i, state[i])


def _sb_prompt_kernel(bias_ref, q_ref, k_ref, v_ref, o_ref, qh_ref, acc_ref, car_ref, *, blk, n_heads):
    hg = pl.program_id(1)
    qi = pl.program_id(2)

    lane = lax.broadcasted_iota(jnp.int32, (blk, LANES), 1)
    first = lane < SB_HEAD_DIM
    for j in range(n_heads):
        qt = q_ref[0, :, (j // 2) * LANES:(j // 2 + 1) * LANES]
        qh_ref[j, :, 0:LANES] = jnp.where(first if j % 2 == 0 else ~first, qt, jnp.zeros_like(qt))
        bias = jnp.full((blk, LANES), bias_ref[n_heads * hg + j], F32)
        hi = bias.astype(BF16).astype(F32)
        qh_ref[j, :, LANES:2 * LANES] = jnp.where(lane == 0, hi, jnp.where(lane == 1, bias - hi, 0.0)).astype(BF16)
    ones_rows = jnp.where(lax.broadcasted_iota(jnp.int32, (LANES, blk), 0) < 2, 1.0, 0.0).astype(BF16)
    upper = _strict_upper(blk)
    ti = lax.broadcasted_iota(jnp.int32, (blk, blk), 0)
    si = lax.broadcasted_iota(jnp.int32, (blk, blk), 1)
    strictly_before = si < ti
    acc_ref[...] = jnp.zeros(acc_ref.shape, F32)
    car_ref[...] = jnp.zeros(car_ref.shape, F32)

    def process(kb, visible):

        def tile(j):
            return slice((j // 2) * LANES, (j // 2 + 1) * LANES)

        def logs(j, _):
            z = _dot(qh_ref[j], jnp.concatenate([k_ref[0, kb, tile(j), :], ones_rows], axis=0))
            return _sb_logs(z, visible)

        def tails(j, st):
            return st + (_dot(st[1].astype(BF16), upper),)

        def weights(j, st):
            a, car_ref[j] = _sb_weights(*st, car_ref[j], visible)
            acc_ref[j] += _dot_nt(a.astype(BF16), v_ref[0, kb, tile(j), :])

        _emit_skewed(n_heads, [logs, tails, weights], lag=PROMPT_STAGE_LAG)

    process(qi, strictly_before)

    def body(i, carry):
        process(qi - 1 - i, None)
        return carry

    lax.fori_loop(0, qi, body, 0)
    for t in range(n_heads // 2):
        o_ref[0, :, t * LANES:(t + 1) * LANES] = jnp.where(first, acc_ref[2 * t], acc_ref[2 * t + 1]).astype(o_ref.dtype)


def _sb_prompt(q, k_t, v_t, bias, blk, n_heads):
    b, l, d = q.shape
    width = n_heads * SB_HEAD_DIM
    q_blk = pl.BlockSpec((1, blk, width), lambda i, hg, qi, bias: (i, qi, hg))
    kv_blk = pl.BlockSpec((1, l // blk, width, blk), lambda i, hg, qi, bias: (i, 0, hg, 0))
    return pl.pallas_call(
        functools.partial(_sb_prompt_kernel, blk=blk, n_heads=n_heads),
        grid_spec=pltpu.PrefetchScalarGridSpec(
            num_scalar_prefetch=1,
            grid=(b, d // width, l // blk),
            in_specs=[q_blk, kv_blk, kv_blk],
            out_specs=q_blk,
            scratch_shapes=[pltpu.VMEM((n_heads, blk, 2 * LANES), BF16),
                            pltpu.VMEM((n_heads, blk, LANES), F32), pltpu.VMEM((n_heads, blk, LANES), F32)]),
        out_shape=jax.ShapeDtypeStruct((b, l, d), BF16),
        compiler_params=_params("parallel", "parallel", "arbitrary"),
        name="sb_prompt",
    )(bias, q, k_t, v_t)


def _sb_sample_kernel(pt_ref, q_ref, kn_ref, vn_ref, bias_ref, *refs, n_step_pages, page, n_heads, single_step):
    k_refs = refs[:n_step_pages]
    v_refs = refs[n_step_pages:2 * n_step_pages]
    o_ref, qbd_ref, kn_buf, vn_buf, acc_ref, car_ref = refs[2 * n_step_pages:]
    del pt_ref
    step = pl.program_id(1)
    n_q, d = q_ref.shape[1], q_ref.shape[2]
    rows = n_q * n_heads
    upper = _strict_upper(page)
    bias = bias_ref[...]
    hrow = lax.broadcasted_iota(jnp.int32, (n_heads, d), 0)
    hlane = lax.broadcasted_iota(jnp.int32, (n_heads, d), 1) // SB_HEAD_DIM
    own_head = hrow == hlane

    def run(blocks):
        def logs(i, _):
            return _sb_logs(blocks[i][0](qbd_ref[...]) + bias, blocks[i][2])

        def tails(i, st):
            return st + (_dot(st[1].astype(BF16), upper),)

        def weights(i, st):
            a, car_ref[...] = _sb_weights(*st, car_ref[...], blocks[i][2])
            acc_ref[...] += blocks[i][1](a.astype(BF16))

        _emit_skewed(len(blocks), [logs, tails, weights], lag=SAMPLE_STAGE_LAG)

    def start():
        q = q_ref[0]
        for qi in range(n_q):
            qrow = jnp.broadcast_to(q[qi:qi + 1, :], (n_heads, d))
            qbd_ref[qi * n_heads:(qi + 1) * n_heads, :] = jnp.where(own_head, qrow, 0.0).astype(BF16)
        acc_ref[...] = jnp.zeros(acc_ref.shape, F32)
        car_ref[...] = jnp.zeros(car_ref.shape, F32)
        kn_buf[...] = jnp.zeros(kn_buf.shape, F32)
        vn_buf[...] = jnp.zeros(vn_buf.shape, F32)
        kn_buf[0:n_q, :] = kn_ref[0]
        vn_buf[0:n_q, :] = vn_ref[0]

    r_q = lax.broadcasted_iota(jnp.int32, (rows, page), 0) // n_heads
    s_k = lax.broadcasted_iota(jnp.int32, (rows, page), 1)
    new_block = (lambda qbd: _dot_nt(qbd, kn_buf[...].astype(BF16)),
                 lambda a: _dot(a, vn_buf[...].astype(BF16)), s_k < r_q)
    pages = [(lambda qbd, r=kr: _dot(qbd, r[0].astype(BF16)), lambda a, r=vr: _dot_nt(a, r[0].astype(BF16)), None)
             for kr, vr in zip(k_refs, v_refs)]
    if single_step:
        start()
        run([new_block] + pages)
    else:
        @pl.when(step == 0)
        def _():
            start()
            run([new_block])

        run(pages)

    @pl.when(step == pl.num_programs(1) - 1)
    def _():
        out_rows = []
        for qi in range(n_q):
            blk = acc_ref[qi * n_heads:(qi + 1) * n_heads, :]
            out_rows.append(jnp.sum(jnp.where(own_head, blk, 0.0), axis=0, keepdims=True))
        o_ref[0] = jnp.concatenate(out_rows, axis=0)


def _sb_sample(q, k_new, v_new, bias_rows, cache_k_t, cache_v_t, page_table, n_step_pages):
    b, n_q, d = q.shape
    n_pages = page_table.shape[1]
    page = cache_k_t.shape[2]
    n_heads = d // SB_HEAD_DIM
    rows = n_q * n_heads
    n_steps = n_pages // n_step_pages
    tok = pl.BlockSpec((1, n_q, d), lambda i, s, pt: (i, 0, 0))

    def page_spec(j):
        return pl.BlockSpec((1, d, page),
                            lambda i, s, pt: (pt[i * n_pages + (n_pages - 1 - (s * n_step_pages + j))], 0, 0))

    pages = [page_spec(j) for j in range(n_step_pages)]
    return pl.pallas_call(
        functools.partial(_sb_sample_kernel, n_step_pages=n_step_pages, page=page, n_heads=n_heads,
                          single_step=n_steps == 1),
        grid_spec=pltpu.PrefetchScalarGridSpec(
            num_scalar_prefetch=1,
            grid=(b, n_steps),
            in_specs=[tok, tok, tok, _resident(bias_rows.shape, lambda i, s, pt: (0, 0))] + pages + pages,
            out_specs=tok,
            scratch_shapes=[pltpu.VMEM((rows, d), BF16), pltpu.VMEM((page, d), F32), pltpu.VMEM((page, d), F32),
                            pltpu.VMEM((rows, d), F32), pltpu.VMEM((rows, LANES), F32)]),
        out_shape=jax.ShapeDtypeStruct((b, n_q, d), F32),
        compiler_params=_params("parallel", "arbitrary"),
        name="sb_sample",
    )(page_table.reshape(-1), q, k_new, v_new, bias_rows,
      *([cache_k_t] * n_step_pages), *([cache_v_t] * n_step_pages))


HGRN_CHUNK = 64
HGRN_SUBCHUNKS = 16
HGRN_STAGE_LAG = 5
HGRN_DECAY_LIMIT = 80.0
HGRN_EXACT_CHUNK = 16
SAMPLE_PAD = 8
SAMPLE_ELEMS = 8
FINAL_TILE = 512
TAIL_GROUPS = 2
FFN_CHUNK = 1024
PROMPT_BLOCK = 256
PROMPT_HEADS = 16
PROMPT_STAGE_LAG = 2
STEP_PAGES = 16
SAMPLE_STAGE_LAG = 3


def _row_tile(t, want):
    return want if t % want == 0 else t


def _trunk(x, p, state0, past, w):
    b, l, d = x.shape
    t = b * l
    x2 = x.reshape(t, d)
    p2 = p.reshape(p.shape[0], t, p.shape[-1])
    q, fg, v, og = _hgrn_in(x2, w["g_mix"][0][None], w["a_lb"], w["w_a_in"], 0, _row_tile(t, 512))
    q, fg, v, og = (a.reshape(b, l, d) for a in (q, fg, v, og))
    if l % HGRN_CHUNK == 0:
        n_sub = HGRN_SUBCHUNKS
        while (l // HGRN_CHUNK) % n_sub:
            n_sub //= 2
        o, s_fin = _hgrn_scan(q, fg, v, og, w["g_a_onorm"][0][None], state0, HGRN_CHUNK, n_sub, 1, BF16)
    else:
        pad = ((0, 0), (0, SAMPLE_PAD - l), (0, 0))
        n_elems = SAMPLE_ELEMS if b % SAMPLE_ELEMS == 0 else 1
        o, s_fin = _hgrn_scan(jnp.pad(q, pad), jnp.pad(fg, pad, constant_values=1.0), jnp.pad(v, pad),
                              jnp.pad(og, pad), w["g_a_onorm"][0][None], state0, SAMPLE_PAD, 1, n_elems, F32)
        o = o[:, :l]
    n_heads = d // SB_HEAD_DIM
    if past is None:
        tm = PROMPT_BLOCK
        h, k_t, v_t, q1, kb_t, vb_t = _layer_tail(x2, o.reshape(t, d), p2, 0, w, tm, final=False, seq_len=l)
        attn = _sb_prompt(q1.reshape(b, l, d), kb_t, vb_t, w["sb_bias"][0], PROMPT_BLOCK, PROMPT_HEADS)
        k, v_kv = (jnp.transpose(a.reshape(b, n_heads, SB_HEAD_DIM, l), (0, 3, 1, 2)) for a in (k_t, v_t))
    else:
        tm = _row_tile(t, 256)
        h, k, v_kv, q1 = _layer_tail(x2, o.reshape(t, d), p2, 0, w, tm, final=False, q_dtype=F32)
        cache_k, cache_v, page_table = past
        bias_rows = jnp.broadcast_to(jnp.tile(w["sb_bias"][0], l)[:, None], (l * n_heads, LANES))
        cache_k_t, cache_v_t = (jnp.transpose(c, (0, 2, 3, 1)).reshape(c.shape[0], d, c.shape[1])
                                for c in (cache_k, cache_v))
        attn = _sb_sample(q1.reshape(b, l, d), k.reshape(b, l, d), v_kv.reshape(b, l, d), bias_rows,
                          cache_k_t, cache_v_t, page_table, min(STEP_PAGES, page_table.shape[1]))
        k, v_kv = (a.reshape(b, l, n_heads, SB_HEAD_DIM) for a in (k, v_kv))
    (y,) = _layer_tail(h, attn.reshape(t, d), p2, 1, w, _row_tile(t, FINAL_TILE), final=True)
    return y.reshape(b, l, d), s_fin[None], k, v_kv


def kernel(x_prompt, x_sample, p_prompt, p_sample, state_hgrn, cache_k, cache_v, page_table, a_lb, w_a_in,
           g_a_onorm, w_a_o, g_kv, w_kv, w_b_q, w_b_o, sb_bias, g_mix, g_ffn, w_ffn_in, w_ffn_out, g_ple,
           w_ple_in, w_ple_gate, g_final):
    w = dict(a_lb=a_lb, g_a_onorm=g_a_onorm, g_kv=g_kv, sb_bias=sb_bias * LOG2E, g_mix=g_mix, g_ffn=g_ffn, g_ple=g_ple,
             g_final=g_final,
             w_a_in=w_a_in.astype(BF16), w_a_o=w_a_o.astype(BF16), w_kv=w_kv.astype(BF16),
             w_kv_t=w_kv.T.astype(BF16),
             w_b_q=w_b_q.astype(BF16), w_b_o=w_b_o.astype(BF16), w_ffn_in=w_ffn_in.astype(BF16),
             w_ffn_out=w_ffn_out.astype(BF16), w_ple_in=w_ple_in.astype(BF16),
             w_ple_gate=w_ple_gate.astype(BF16))
    y_p, st_p, k_p, v_p = _trunk(x_prompt, p_prompt, None, None, w)
    y_s, st_s, k_s, v_s = _trunk(x_sample, p_sample, state_hgrn[0], (cache_k, cache_v, page_table), w)
    return (y_p, y_s, st_p, st_s, k_p, v_p, k_s, v_s)
```
